```python
import jax, jax.numpy as jnp
from jax import lax
import numpy as np

D_MODEL = 1024
BATCH = 4
SEQ = 4096
DEPTH = 4

CHUNK = 64
N_MIXERS = 3
N_HEADS = 16
HEAD_DIM = D_MODEL // N_HEADS
Q_BLOCK = 128
SG_CHUNK = 128
SG_WIDTH = D_MODEL
SG_GROUPS = 8
SG_GROUP_DIM = SG_WIDTH // SG_GROUPS
CONV_WIDTH = 31
D_FF = 4 * D_MODEL
N_A = (DEPTH + 2) // N_MIXERS
N_B = (DEPTH + 1) // N_MIXERS
N_C = DEPTH // N_MIXERS
EPS = 1e-6

kernel_name = "chunk_causal_hybrid_fox_gmlp_conformer"


def rms_norm(x, g):
    xf = x.astype(jnp.float32)
    y = xf * lax.rsqrt(jnp.mean(xf * xf, axis=-1, keepdims=True) + EPS)
    return (y * g.astype(jnp.float32)).astype(x.dtype)


def layer_norm(x, g, b):
    xf = x.astype(jnp.float32)
    mu = jnp.mean(xf, axis=-1, keepdims=True)
    xc = xf - mu
    y = xc * lax.rsqrt(jnp.mean(xc * xc, axis=-1, keepdims=True) + EPS)
    return (y * g.astype(jnp.float32) + b.astype(jnp.float32)).astype(x.dtype)


def fox_mixer(h, w_in, b_f, q_g, k_g, w_out):
    B, S, D = h.shape
    proj = h @ w_in
    q, k, v, f_pre = jnp.split(proj, [D, 2 * D, 3 * D], axis=-1)
    q = rms_norm(q.reshape(B, S, N_HEADS, HEAD_DIM), q_g)
    k = rms_norm(k.reshape(B, S, N_HEADS, HEAD_DIM), k_g)
    v = v.reshape(B, S, N_HEADS, HEAD_DIM)
    log_f = jax.nn.log_sigmoid(f_pre.astype(jnp.float32) + b_f.astype(jnp.float32))
    F = jnp.cumsum(log_f, axis=1).transpose(0, 2, 1)
    nb = S // Q_BLOCK
    qb = q.reshape(B, nb, Q_BLOCK, N_HEADS, HEAD_DIM).swapaxes(0, 1)
    Fq = F.reshape(B, N_HEADS, nb, Q_BLOCK).transpose(2, 0, 1, 3)
    k_pos = jnp.arange(S)
    scale = HEAD_DIM ** -0.5

    def block(args):
        q_i, F_i, b_i = args
        logits = (jnp.einsum('bqhd,bkhd->bhqk', q_i, k).astype(jnp.float32) * scale
                  + (F_i[..., :, None] - F[..., None, :]))
        q_pos = b_i * Q_BLOCK + jnp.arange(Q_BLOCK)
        logits = jnp.where(k_pos[None, :] <= q_pos[:, None], logits, -jnp.inf)
        p = jax.nn.softmax(logits, axis=-1).astype(v.dtype)
        return jnp.einsum('bhqk,bkhd->bqhd', p, v)

    o = lax.map(block, (qb, Fq, jnp.arange(nb)))
    o = o.swapaxes(0, 1).reshape(B, S, D)
    return o @ w_out


def gmlp_mixer(h, w_in, ln_g, ln_b, w_s, b_s, w_out):
    B, S, _ = h.shape
    uv = jax.nn.gelu(h @ w_in)
    u, v = jnp.split(uv, 2, axis=-1)
    v = layer_norm(v, ln_g, ln_b)
    v = v.reshape(B, S // SG_CHUNK, SG_CHUNK, SG_GROUPS, SG_GROUP_DIM)
    cid = jnp.arange(SG_CHUNK) // CHUNK
    mask = cid[None, :] <= cid[:, None]
    ws = jnp.where(mask[None], w_s, jnp.zeros_like(w_s))
    v = jnp.einsum('gts,bnsgc->bntgc', ws, v) + b_s.T[:, :, None]
    v = v.reshape(B, S, SG_WIDTH)
    return (u * v) @ w_out


def conv_mixer(h, w_pw1, b_pw1, w_dw, b_dw, ln_g, ln_b, w_pw2, b_pw2):
    D = h.shape[-1]
    y = jax.nn.glu(h @ w_pw1 + b_pw1, axis=-1)
    y = lax.conv_general_dilated(y, w_dw[:, None, :], window_strides=(1,),
                                 padding=[(CONV_WIDTH - 1, 0)],
                                 dimension_numbers=('NWC', 'WIO', 'NWC'),
                                 feature_group_count=D) + b_dw
    y = jax.nn.silu(layer_norm(y, ln_g, ln_b))
    return y @ w_pw2 + b_pw2


def setup_inputs(seed: int = 0) -> dict:
    key = jax.random.key(seed)
    ks = iter(jax.random.split(key, 40))
    D = D_MODEL

    def nrm(shape, scale):
        return jax.random.normal(next(ks), shape, jnp.float32) * scale

    def gain(shape):
        return 1.0 + nrm(shape, 0.05)

    return {
        "x": nrm((BATCH, SEQ, D), 1.0),
        "c": nrm((BATCH, D), 1.0),
        "norm_mix": gain((DEPTH, D)),
        "norm_mlp": gain((DEPTH, D)),
        "w_ada": nrm((DEPTH, D, 6 * D), 0.5 * D ** -0.5),
        "b_ada": nrm((DEPTH, 6 * D), 0.02),
        "w_mlp_in": nrm((DEPTH, D, D_FF), D ** -0.5),
        "w_mlp_out": nrm((DEPTH, D_FF, D), D_FF ** -0.5),
        "fox_w_in": nrm((N_A, D, 3 * D + N_HEADS), D ** -0.5),
        "fox_b_f": jax.random.uniform(next(ks), (N_A, N_HEADS), jnp.float32, 1.0, 6.0),
        "fox_q_norm": gain((N_A, HEAD_DIM)),
        "fox_k_norm": gain((N_A, HEAD_DIM)),
        "fox_w_out": nrm((N_A, D, D), D ** -0.5),
        "sg_w_in": nrm((N_B, D, 2 * SG_WIDTH), D ** -0.5),
        "sg_ln_g": gain((N_B, SG_WIDTH)),
        "sg_ln_b": nrm((N_B, SG_WIDTH), 0.02),
        "sg_w_s": nrm((N_B, SG_GROUPS, SG_CHUNK, SG_CHUNK), 0.5 * SG_CHUNK ** -0.5),
        "sg_b_s": 1.0 + nrm((N_B, SG_GROUPS, SG_CHUNK), 0.02),
        "sg_w_out": nrm((N_B, SG_WIDTH, D), SG_WIDTH ** -0.5),
        "cv_w_pw1": nrm((N_C, D, 2 * D), D ** -0.5),
        "cv_b_pw1": nrm((N_C, 2 * D), 0.02),
        "cv_w_dw": nrm((N_C, CONV_WIDTH, D), CONV_WIDTH ** -0.5),
        "cv_b_dw": nrm((N_C, D), 0.02),
        "cv_ln_g": gain((N_C, D)),
        "cv_ln_b": nrm((N_C, D), 0.02),
        "cv_w_pw2": nrm((N_C, D, D), D ** -0.5),
        "cv_b_pw2": nrm((N_C, D), 0.02),
    }


def reference(x, c, norm_mix, norm_mlp, w_ada, b_ada, w_mlp_in, w_mlp_out,
              fox_w_in, fox_b_f, fox_q_norm, fox_k_norm, fox_w_out,
              sg_w_in, sg_ln_g, sg_ln_b, sg_w_s, sg_b_s, sg_w_out,
              cv_w_pw1, cv_b_pw1, cv_w_dw, cv_b_dw, cv_ln_g, cv_ln_b, cv_w_pw2, cv_b_pw2):
    c_act = jax.nn.silu(c)
    for i in range(DEPTH):
        kind = i % N_MIXERS
        j = i // N_MIXERS
        mod = c_act @ w_ada[i] + b_ada[i]
        sh_m, sc_m, g_m, sh_f, sc_f, g_f = [m[:, None, :] for m in jnp.split(mod, 6, axis=-1)]
        h = rms_norm(x, norm_mix[i]) * (1 + sc_m) + sh_m
        if kind == 0:
            y = fox_mixer(h, fox_w_in[j], fox_b_f[j], fox_q_norm[j], fox_k_norm[j], fox_w_out[j])
        elif kind == 1:
            y = gmlp_mixer(h, sg_w_in[j], sg_ln_g[j], sg_ln_b[j], sg_w_s[j], sg_b_s[j], sg_w_out[j])
        else:
            y = conv_mixer(h, cv_w_pw1[j], cv_b_pw1[j], cv_w_dw[j], cv_b_dw[j],
                           cv_ln_g[j], cv_ln_b[j], cv_w_pw2[j], cv_b_pw2[j])
        x = x + g_m * y
        h = rms_norm(x, norm_mlp[i]) * (1 + sc_f) + sh_f
        x = x + g_f * (jnp.square(jax.nn.relu(h @ w_mlp_in[i])) @ w_mlp_out[i])
    return x
```

```python
import functools

import jax
import jax.numpy as jnp
import numpy as np
from jax import lax
from jax.experimental import pallas as pl
from jax.experimental.pallas import tpu as pltpu

F32 = jnp.float32
BF16 = jnp.bfloat16

D = 1024
N_HEADS = 16
DH = D // N_HEADS
D_FF = 4 * D
EPS = 1e-6
SG_CHUNK = 128
SG_BLOCK = 64
SG_GROUPS = 8
CONV_W = 31
HALO = 32

LANES = 128
MXU = 256
TM = 512
TQ = 256
TK = 256
FF_CHUNK = 1024
N_SPLIT = 3
VMEM_LIMIT = 56 * 1024 * 1024


def _const_spec(shape):
    nd = len(shape)
    return pl.BlockSpec(shape, lambda *_: (0,) * nd, pipeline_mode=pl.Buffered(1))


def _params():
    return pltpu.CompilerParams(
        dimension_semantics=("arbitrary", "arbitrary"), vmem_limit_bytes=VMEM_LIMIT)


def _dot(a, b):
    return jnp.dot(a, b, preferred_element_type=F32)


def _sigmoid(x):
    return 1.0 / (1.0 + jnp.exp(-x))


def _rms_mod(x, gain, scale, shift):
    ms = jnp.mean(x * x, axis=-1, keepdims=True)
    return (x * lax.rsqrt(ms + EPS) * gain) * (1.0 + scale) + shift


def _layer_norm(x, g, b):
    mu = jnp.mean(x, axis=-1, keepdims=True)
    xc = x - mu
    var = jnp.mean(xc * xc, axis=-1, keepdims=True)
    return xc * lax.rsqrt(var + EPS) * g + b


def _split_bf16(x):
    pieces = []
    r = x
    for _ in range(N_SPLIT):
        p = r.astype(BF16)
        pieces.append(p)
        r = r - p.astype(F32)
    return pieces


def _mlp_tail(x1, mod, nmlp, w1_ref, w2_ref):
    h = _rms_mod(x1, nmlp, mod[4:5], mod[3:4]).astype(BF16)
    acc = jnp.zeros(x1.shape, F32)
    for c in range(D_FF // FF_CHUNK):
        lo = c * FF_CHUNK
        hid = _dot(h, w1_ref[:, lo:lo + FF_CHUNK])
        hid = jnp.square(jnp.maximum(hid, 0.0)).astype(BF16)
        acc = acc + _dot(hid, w2_ref[lo:lo + FF_CHUNK, :])
    return x1 + mod[5:6] * acc


def _ada_kernel(c_ref, w_ref, b_ref, o_ref):
    c = c_ref[...]
    ca = c * _sigmoid(c)
    o_ref[0] = jnp.dot(ca, w_ref[0], preferred_element_type=F32,
                       precision=lax.Precision.HIGHEST) + b_ref[0]


def _ada(c, w_ada, b_ada):
    depth, _, n = w_ada.shape
    bsz = c.shape[0]
    rows = 8
    tn = 1536
    cp = jnp.zeros((rows, D), F32).at[:bsz].set(c)
    out = pl.pallas_call(
        _ada_kernel,
        grid=(depth, n // tn),
        in_specs=[
            pl.BlockSpec((rows, D), lambda i, j: (0, 0)),
            pl.BlockSpec((1, D, tn), lambda i, j: (i, 0, j)),
            pl.BlockSpec((1, 1, tn), lambda i, j: (i, 0, j)),
        ],
        out_specs=pl.BlockSpec((1, rows, tn), lambda i, j: (i, 0, j)),
        out_shape=jax.ShapeDtypeStruct((depth, rows, n), F32),
        compiler_params=_params(),
        name="ada_mod",
    )(cp, w_ada, b_ada.reshape(depth, 1, n))
    return out[:, :bsz].reshape(depth, bsz, 6, D)


def _sg_kernel(x_ref, mod_ref, nmix_ref, nmlp_ref, win_ref, lng_ref, lnb_ref, ws_ref, bs_ref,
               wout_ref, w1_ref, w2_ref, o_ref, gated_ref):
    x = x_ref[0]
    mod = mod_ref[0]
    h = _rms_mod(x, nmix_ref[...], mod[1:2], mod[0:1]).astype(BF16)
    uv = jax.nn.gelu(_dot(h, win_ref[...]), approximate=True)
    u = uv[:, :D]
    v = _layer_norm(uv[:, D:], lng_ref[...], lnb_ref[...]).astype(BF16)

    t = lax.broadcasted_iota(jnp.int32, (SG_CHUNK, SG_CHUNK), 0) // SG_BLOCK
    s = lax.broadcasted_iota(jnp.int32, (SG_CHUNK, SG_CHUNK), 1) // SG_BLOCK
    causal = s <= t
    gd = D // SG_GROUPS
    for g in range(SG_GROUPS):
        ws = jnp.where(causal, ws_ref[g], 0.0).astype(BF16)
        bias = bs_ref[g]
        cols = slice(g * gd, (g + 1) * gd)
        for j in range(TM // (2 * SG_CHUNK)):
            r0 = j * 2 * SG_CHUNK
            r1 = r0 + SG_CHUNK
            r2 = r1 + SG_CHUNK
            rhs = jnp.concatenate([v[r0:r1, cols], v[r1:r2, cols]], axis=1)
            mix = _dot(ws, rhs)
            gated_ref[r0:r1, cols] = (u[r0:r1, cols] * (mix[:, :gd] + bias)).astype(BF16)
            gated_ref[r1:r2, cols] = (u[r1:r2, cols] * (mix[:, gd:] + bias)).astype(BF16)

    x1 = x + mod[2:3] * _dot(gated_ref[...], wout_ref[...])
    o_ref[0] = _mlp_tail(x1, mod, nmlp_ref[...], w1_ref, w2_ref)


def _sg_layer(x, mod, nmix, nmlp, w_in, ln_g, ln_b, w_s, b_s, w_out, w1, w2):
    bsz, seq, _ = x.shape
    tok = pl.BlockSpec((1, TM, D), lambda b, s: (b, s, 0))
    bs_x = jnp.broadcast_to(b_s[:, :, None], (SG_GROUPS, SG_CHUNK, D // SG_GROUPS))
    return pl.pallas_call(
        _sg_kernel,
        grid=(bsz, seq // TM),
        in_specs=[
            tok,
            pl.BlockSpec((1, 6, D), lambda b, s: (b, 0, 0)),
            _const_spec((1, D)), _const_spec((1, D)),
            _const_spec((D, 2 * D)), _const_spec((1, D)), _const_spec((1, D)),
            _const_spec((SG_GROUPS, SG_CHUNK, SG_CHUNK)),
            _const_spec((SG_GROUPS, SG_CHUNK, D // SG_GROUPS)),
            _const_spec((D, D)), _const_spec((D, D_FF)), _const_spec((D_FF, D)),
        ],
        out_specs=tok,
        out_shape=jax.ShapeDtypeStruct(x.shape, F32),
        scratch_shapes=[pltpu.VMEM((TM, D), BF16)],
        compiler_params=_params(),
        name="sg_layer",
    )(x, mod, nmix.reshape(1, D), nmlp.reshape(1, D), w_in.astype(BF16),
      ln_g.reshape(1, D), ln_b.reshape(1, D), w_s, bs_x, w_out.astype(BF16),
      w1.astype(BF16), w2.astype(BF16))


def _cv_kernel(x_ref, mod_ref, nmix_ref, nmlp_ref, wpw1_ref, bpw1_ref, wdw_ref, bdw_ref,
               lng_ref, lnb_ref, wpw2_ref, bpw2_ref, w1_ref, w2_ref, o_ref, ybuf_ref, conv_ref):
    x = x_ref[0]
    mod = mod_ref[0]
    h = _rms_mod(x, nmix_ref[...], mod[1:2], mod[0:1]).astype(BF16)
    yz = _dot(h, wpw1_ref[...]) + bpw1_ref[...]
    y = yz[:, :D] * _sigmoid(yz[:, D:])

    @pl.when(pl.program_id(1) == 0)
    def _():
        ybuf_ref[0:HALO, :] = jnp.zeros((HALO, D), F32)

    ybuf_ref[HALO:HALO + TM, :] = y

    first = HALO - (CONV_W - 1)
    n_a = (CONV_W + 7) // 8
    rb = 128
    for r in range(TM // rb):
        for c in range(D // LANES):
            cols = slice(c * LANES, (c + 1) * LANES)
            win = ybuf_ref[r * rb:r * rb + rb + HALO, cols]
            acc = jnp.zeros((rb, LANES), F32) + bdw_ref[:, cols]
            for b in range(8):
                taps = [a for a in range(n_a) if 8 * a + b < CONV_W]
                zb = win[first + b:first + b + rb + 8 * taps[-1], :]
                for a in taps:
                    j = 8 * a + b
                    acc = acc + wdw_ref[j:j + 1, cols] * zb[8 * a:8 * a + rb, :]
            conv_ref[r * rb:(r + 1) * rb, cols] = acc

    ybuf_ref[0:HALO, :] = ybuf_ref[TM:TM + HALO, :]

    z = _layer_norm(conv_ref[...], lng_ref[...], lnb_ref[...])
    z = (z * _sigmoid(z)).astype(BF16)
    x1 = x + mod[2:3] * (_dot(z, wpw2_ref[...]) + bpw2_ref[...])
    o_ref[0] = _mlp_tail(x1, mod, nmlp_ref[...], w1_ref, w2_ref)


def _cv_layer(x, mod, nmix, nmlp, w_pw1, b_pw1, w_dw, b_dw, ln_g, ln_b, w_pw2, b_pw2, w1, w2):
    bsz, seq, _ = x.shape
    tok = pl.BlockSpec((1, TM, D), lambda b, s: (b, s, 0))
    wdw = jnp.zeros((HALO, D), F32).at[:CONV_W].set(w_dw)
    return pl.pallas_call(
        _cv_kernel,
        grid=(bsz, seq // TM),
        in_specs=[
            tok,
            pl.BlockSpec((1, 6, D), lambda b, s: (b, 0, 0)),
            _const_spec((1, D)), _const_spec((1, D)),
            _const_spec((D, 2 * D)), _const_spec((1, 2 * D)),
            _const_spec((HALO, D)), _const_spec((1, D)),
            _const_spec((1, D)), _const_spec((1, D)),
            _const_spec((D, D)), _const_spec((1, D)),
            _const_spec((D, D_FF)), _const_spec((D_FF, D)),
        ],
        out_specs=tok,
        out_shape=jax.ShapeDtypeStruct(x.shape, F32),
        scratch_shapes=[pltpu.VMEM((TM + HALO, D), F32), pltpu.VMEM((TM, D), F32)],
        compiler_params=_params(),
        name="cv_layer",
    )(x, mod, nmix.reshape(1, D), nmlp.reshape(1, D), w_pw1.astype(BF16), b_pw1.reshape(1, 2 * D),
      wdw, b_dw.reshape(1, D), ln_g.reshape(1, D), ln_b.reshape(1, D),
      w_pw2.astype(BF16), b_pw2.reshape(1, D), w1.astype(BF16), w2.astype(BF16))


def _fox_proj_kernel(x_ref, mod_ref, nmix_ref, wqv_ref, wk_ref, wf_ref, bf_ref, gqk_ref, gsum_ref,
                     tri_ref, sel_ref, qT_ref, vT_ref, k2_ref, aug_ref, carry_ref):
    x = x_ref[0]
    mod = mod_ref[0]
    h = _rms_mod(x, nmix_ref[...], mod[1:2], mod[0:1]).astype(BF16)

    qvT = lax.dot_general(wqv_ref[...], h, (((1,), (1,)), ((), ())), preferred_element_type=F32)
    qT = qvT[:D].reshape(N_HEADS, DH, TM)
    qT = qT * lax.rsqrt(jnp.mean(qT * qT, axis=1, keepdims=True) + EPS)
    qT = qT.reshape(D, TM).astype(BF16)
    vT = qvT[D:].astype(BF16)
    for j in range(TM // TQ):
        qT_ref[0, j] = qT[:, j * TQ:(j + 1) * TQ]
    for j in range(TM // TK):
        vT_ref[0, j] = vT[:, j * TK:(j + 1) * TK]

    k = _dot(h, wk_ref[...])
    kk = (k * k).astype(BF16)
    ss = jnp.concatenate(
        [_dot(kk[:, c * MXU:(c + 1) * MXU], gsum_ref[...]) for c in range(D // MXU)], axis=1)
    k2_ref[0] = (k * lax.rsqrt(ss * (1.0 / DH) + EPS) * gqk_ref[...]).astype(BF16)

    f_pre = _dot(h, wf_ref[...]) + bf_ref[...]
    log_f = jnp.minimum(f_pre, 0.0) - jnp.log1p(jnp.exp(-jnp.abs(f_pre)))

    @pl.when(pl.program_id(1) == 0)
    def _():
        carry_ref[...] = jnp.zeros(carry_ref.shape, F32)

    cum = carry_ref[0:1, :]
    for p in _split_bf16(log_f):
        cum = cum + _dot(tri_ref[...], p)
    carry_ref[0:1, :] = cum[TM - 1:TM, :]

    aug = jnp.zeros((TM, D), F32)
    for i, p in enumerate(_split_bf16(-cum)):
        aug = aug + _dot(p, sel_ref[i])
    aug_ref[0] = aug.astype(BF16)


def _fox_consts():
    g = np.kron(np.eye(MXU // DH), np.ones((DH, DH)))
    tri = np.tril(np.ones((TM, TM)))
    sel = np.zeros((N_SPLIT, LANES, D))
    for i in range(N_SPLIT):
        for hd in range(N_HEADS):
            sel[i, hd, LANES * (hd // 2) + N_SPLIT * (hd % 2) + i] = 1.0
    return (jnp.asarray(g, BF16), jnp.asarray(tri, BF16), jnp.asarray(sel, BF16))


def _fox_proj(x, mod, nmix, w_in, b_f, q_g, k_g):
    bsz, seq, _ = x.shape
    gsum, tri, sel = _fox_consts()
    wqv = jnp.concatenate([w_in[:, :D].T, w_in[:, 2 * D:3 * D].T], axis=0).astype(BF16)
    wk = w_in[:, D:2 * D].astype(BF16)
    wf = jnp.zeros((D, LANES), F32).at[:, :N_HEADS].set(w_in[:, 3 * D:]).astype(BF16)
    bf = jnp.zeros((1, LANES), F32).at[0, :N_HEADS].set(b_f)
    gqk = jnp.tile(q_g * k_g * (DH ** -0.5), N_HEADS).reshape(1, D)
    tok = pl.BlockSpec((1, TM, D), lambda b, s: (b, s, 0))
    return pl.pallas_call(
        _fox_proj_kernel,
        grid=(bsz, seq // TM),
        in_specs=[
            tok,
            pl.BlockSpec((1, 6, D), lambda b, s: (b, 0, 0)),
            _const_spec((1, D)),
            _const_spec((2 * D, D)), _const_spec((D, D)), _const_spec((D, LANES)),
            _const_spec((1, LANES)), _const_spec((1, D)), _const_spec((MXU, MXU)),
            _const_spec((TM, TM)), _const_spec((N_SPLIT, LANES, D)),
        ],
        out_specs=[
            pl.BlockSpec((1, TM // TQ, D, TQ), lambda b, s: (b, s, 0, 0)),
            pl.BlockSpec((1, TM // TK, D, TK), lambda b, s: (b, s, 0, 0)),
            tok, tok,
        ],
        out_shape=[
            jax.ShapeDtypeStruct((bsz, seq // TQ, D, TQ), BF16),
            jax.ShapeDtypeStruct((bsz, seq // TK, D, TK), BF16),
            jax.ShapeDtypeStruct((bsz, seq, D), BF16),
            jax.ShapeDtypeStruct((bsz, seq, D), BF16),
        ],
        scratch_shapes=[pltpu.VMEM((8, LANES), F32)],
        compiler_params=_params(),
        name="fox_proj",
    )(x, mod, nmix.reshape(1, D), wqv, wk, wf, bf, gqk, gsum, tri, sel)


def _fox_attn_kernel(qT_ref, k_ref, aug_ref, vT_ref, o_ref):
    qi = pl.program_id(2)
    q2 = qT_ref[0, 0]
    zeros = jnp.zeros((DH, TQ), BF16)
    row = lax.broadcasted_iota(jnp.int32, (LANES, TQ), 0)
    qpads = []
    for hd in range(2):
        pick = (row >= N_SPLIT * hd) & (row < N_SPLIT * (hd + 1))
        ones = jnp.where(pick, 1.0, 0.0).astype(BF16)
        parts = [q2[:DH], zeros] if hd == 0 else [zeros, q2[DH:]]
        qpads.append(jnp.concatenate(parts + [ones], axis=0))
    ones_rows = jnp.ones((16, TK), BF16)
    key_le_query = (lax.broadcasted_iota(jnp.int32, (TK, TQ), 0)
                    <= lax.broadcasted_iota(jnp.int32, (TK, TQ), 1))

    def tile(ki, carry, masked):
        k0 = pl.multiple_of(ki * TK, TK)
        kk = jnp.concatenate([k_ref[0, pl.ds(k0, TK), :], aug_ref[0, pl.ds(k0, TK), :]], axis=1)
        v2 = vT_ref[0, ki]
        out = []
        for hd in range(2):
            m_old, acc = carry[hd]
            sT = _dot(kk, qpads[hd])
            if masked:
                sT = jnp.where(key_le_query, sT, -jnp.inf)
            m_new = jnp.maximum(m_old, jnp.max(sT, axis=0, keepdims=True))
            p = jnp.exp(sT - m_new).astype(BF16)
            va = jnp.concatenate([v2[hd * DH:(hd + 1) * DH], ones_rows], axis=0)
            acc = acc * jnp.exp(m_old - m_new) + _dot(va, p)
            out.append((m_new, acc))
        return tuple(out)

    init = tuple((jnp.full((1, TQ), -1e30, F32), jnp.zeros((DH + 16, TQ), F32)) for _ in range(2))
    carry = lax.fori_loop(0, qi, lambda ki, c: tile(ki, c, False), init)
    carry = tile(qi, carry, True)

    oT = jnp.concatenate([acc[:DH] / acc[DH:DH + 1] for _, acc in carry], axis=0)
    o_ref[0] = oT.T.astype(BF16)


def _fox_attn(qT, k2, aug, vT):
    bsz, nq, _, _ = qT.shape
    seq = k2.shape[1]
    assert TQ == TK
    return pl.pallas_call(
        _fox_attn_kernel,
        grid=(bsz, N_HEADS // 2, nq),
        in_specs=[
            pl.BlockSpec((1, 1, LANES, TQ), lambda b, p, q: (b, q, p, 0)),
            pl.BlockSpec((1, seq, LANES), lambda b, p, q: (b, 0, p)),
            pl.BlockSpec((1, seq, LANES), lambda b, p, q: (b, 0, p)),
            pl.BlockSpec((1, seq // TK, LANES, TK), lambda b, p, q: (b, 0, p, 0)),
        ],
        out_specs=pl.BlockSpec((1, TQ, LANES), lambda b, p, q: (b, q, p)),
        out_shape=jax.ShapeDtypeStruct((bsz, seq, D), BF16),
        compiler_params=pltpu.CompilerParams(
            dimension_semantics=("arbitrary", "arbitrary", "arbitrary"),
            vmem_limit_bytes=VMEM_LIMIT),
        name="fox_attn",
    )(qT, k2, aug, vT)


def _fox_tail_kernel(x_ref, o_ref_in, mod_ref, nmlp_ref, wout_ref, w1_ref, w2_ref, out_ref):
    x = x_ref[0]
    mod = mod_ref[0]
    x1 = x + mod[2:3] * _dot(o_ref_in[0], wout_ref[...])
    out_ref[0] = _mlp_tail(x1, mod, nmlp_ref[...], w1_ref, w2_ref)


def _fox_tail(x, o, mod, nmlp, w_out, w1, w2):
    bsz, seq, _ = x.shape
    tok = pl.BlockSpec((1, TM, D), lambda b, s: (b, s, 0))
    return pl.pallas_call(
        _fox_tail_kernel,
        grid=(bsz, seq // TM),
        in_specs=[
            tok, tok,
            pl.BlockSpec((1, 6, D), lambda b, s: (b, 0, 0)),
            _const_spec((1, D)),
            _const_spec((D, D)), _const_spec((D, D_FF)), _const_spec((D_FF, D)),
        ],
        out_specs=tok,
        out_shape=jax.ShapeDtypeStruct(x.shape, F32),
        compiler_params=_params(),
        name="fox_tail",
    )(x, o, mod, nmlp.reshape(1, D), w_out.astype(BF16), w1.astype(BF16), w2.astype(BF16))


def kernel(x, c, norm_mix, norm_mlp, w_ada, b_ada, w_mlp_in, w_mlp_out, fox_w_in, fox_b_f, fox_q_norm, fox_k_norm, fox_w_out, sg_w_in, sg_ln_g, sg_ln_b, sg_w_s, sg_b_s, sg_w_out, cv_w_pw1, cv_b_pw1, cv_w_dw, cv_b_dw, cv_ln_g, cv_ln_b, cv_w_pw2, cv_b_pw2):
    depth = w_ada.shape[0]
    assert x.shape[1] % TM == 0 and x.shape[2] == D
    mods = _ada(c, w_ada, b_ada)
    for i in range(depth):
        kind, j = i % 3, i // 3
        mod = mods[i]
        if kind == 0:
            qT, vT, k2, aug = _fox_proj(x, mod, norm_mix[i], fox_w_in[j], fox_b_f[j],
                                        fox_q_norm[j], fox_k_norm[j])
            o = _fox_attn(qT, k2, aug, vT)
            x = _fox_tail(x, o, mod, norm_mlp[i], fox_w_out[j], w_mlp_in[i], w_mlp_out[i])
        elif kind == 1:
            x = _sg_layer(x, mod, norm_mix[i], norm_mlp[i], sg_w_in[j], sg_ln_g[j], sg_ln_b[j],
                          sg_w_s[j], sg_b_s[j], sg_w_out[j], w_mlp_in[i], w_mlp_out[i])
        else:
            x = _cv_layer(x, mod, norm_mix[i], norm_mlp[i], cv_w_pw1[j], cv_b_pw1[j], cv_w_dw[j],
                          cv_b_dw[j], cv_ln_g[j], cv_ln_b[j], cv_w_pw2[j], cv_b_pw2[j],
                          w_mlp_in[i], w_mlp_out[i])
    return x
```

```python
import functools

import jax
import jax.numpy as jnp
import numpy as np
from jax import lax
from jax.experimental import pallas as pl
from jax.experimental.pallas import tpu as pltpu

F32 = jnp.float32
BF16 = jnp.bfloat16

D = 1024
N_HEADS = 16
DH = D // N_HEADS
D_FF = 4 * D
EPS = 1e-6
LOG2E = 1.4426950408889634
SG_CHUNK = 128
SG_BLOCK = 64
SG_GROUPS = 8
CONV_W = 31
HALO = 32

LANES = 128
MXU = 256
TM = 512
TQ = 256
TK = 256
HPS = 4
FF_CHUNK = 1024
N_SPLIT = 3
VMEM_LIMIT = 56 * 1024 * 1024


def _const_spec(shape):
    nd = len(shape)
    return pl.BlockSpec(shape, lambda *_: (0,) * nd, pipeline_mode=pl.Buffered(1))


def _params():
    return pltpu.CompilerParams(
        dimension_semantics=("arbitrary", "arbitrary"), vmem_limit_bytes=VMEM_LIMIT)


def _dot(a, b):
    return jnp.dot(a, b, preferred_element_type=F32)


def _sigmoid(x):
    return 1.0 / (1.0 + jnp.exp(-x))


def _rms_mod(x, gain, scale, shift):
    ms = jnp.mean(x * x, axis=-1, keepdims=True)
    return (x * lax.rsqrt(ms + EPS) * gain) * (1.0 + scale) + shift


def _layer_norm(x, g, b):
    mu = jnp.mean(x, axis=-1, keepdims=True)
    xc = x - mu
    var = jnp.mean(xc * xc, axis=-1, keepdims=True)
    return xc * lax.rsqrt(var + EPS) * g + b


def _split_bf16(x):
    pieces = []
    r = x
    for _ in range(N_SPLIT):
        p = r.astype(BF16)
        pieces.append(p)
        r = r - p.astype(F32)
    return pieces


def _mlp_tail(x1, mod, nmlp, w1_ref, w2_ref):
    h = _rms_mod(x1, nmlp, mod[4:5], mod[3:4]).astype(BF16)
    acc = jnp.zeros(x1.shape, F32)
    for c in range(D_FF // FF_CHUNK):
        lo = c * FF_CHUNK
        hid = _dot(h, w1_ref[:, lo:lo + FF_CHUNK])
        hid = jnp.square(jnp.maximum(hid, 0.0)).astype(BF16)
        acc = acc + _dot(hid, w2_ref[lo:lo + FF_CHUNK, :])
    return x1 + mod[5:6] * acc


def _ada_kernel(c_ref, w_ref, b_ref, o_ref):
    c = c_ref[...]
    ca = c * _sigmoid(c)
    o_ref[0] = jnp.dot(ca, w_ref[0], preferred_element_type=F32,
                       precision=lax.Precision.HIGHEST) + b_ref[0]


def _ada(c, w_ada, b_ada):
    depth, _, n = w_ada.shape
    bsz = c.shape[0]
    rows = 8
    tn = 1536
    cp = jnp.zeros((rows, D), F32).at[:bsz].set(c)
    out = pl.pallas_call(
        _ada_kernel,
        grid=(depth, n // tn),
        in_specs=[
            pl.BlockSpec((rows, D), lambda i, j: (0, 0)),
            pl.BlockSpec((1, D, tn), lambda i, j: (i, 0, j)),
            pl.BlockSpec((1, 1, tn), lambda i, j: (i, 0, j)),
        ],
        out_specs=pl.BlockSpec((1, rows, tn), lambda i, j: (i, 0, j)),
        out_shape=jax.ShapeDtypeStruct((depth, rows, n), F32),
        compiler_params=_params(),
        name="ada_mod",
    )(cp, w_ada, b_ada.reshape(depth, 1, n))
    return out[:, :bsz].reshape(depth, bsz, 6, D)


def _sg_kernel(x_ref, mod_ref, nmix_ref, nmlp_ref, win_ref, lng_ref, lnb_ref, ws_ref, bs_ref,
               wout_ref, w1_ref, w2_ref, o_ref, gated_ref):
    x = x_ref[0]
    mod = mod_ref[0]
    h = _rms_mod(x, nmix_ref[...], mod[1:2], mod[0:1]).astype(BF16)
    uv = jax.nn.gelu(_dot(h, win_ref[...]), approximate=True)
    u = uv[:, :D]
    v = _layer_norm(uv[:, D:], lng_ref[...], lnb_ref[...]).astype(BF16)

    t = lax.broadcasted_iota(jnp.int32, (SG_CHUNK, SG_CHUNK), 0) // SG_BLOCK
    s = lax.broadcasted_iota(jnp.int32, (SG_CHUNK, SG_CHUNK), 1) // SG_BLOCK
    causal = s <= t
    gd = D // SG_GROUPS
    for g in range(SG_GROUPS):
        ws = jnp.where(causal, ws_ref[g], 0.0).astype(BF16)
        bias = bs_ref[g]
        cols = slice(g * gd, (g + 1) * gd)
        for j in range(TM // (2 * SG_CHUNK)):
            r0 = j * 2 * SG_CHUNK
            r1 = r0 + SG_CHUNK
            r2 = r1 + SG_CHUNK
            rhs = jnp.concatenate([v[r0:r1, cols], v[r1:r2, cols]], axis=1)
            mix = _dot(ws, rhs)
            gated_ref[r0:r1, cols] = (u[r0:r1, cols] * (mix[:, :gd] + bias)).astype(BF16)
            gated_ref[r1:r2, cols] = (u[r1:r2, cols] * (mix[:, gd:] + bias)).astype(BF16)

    x1 = x + mod[2:3] * _dot(gated_ref[...], wout_ref[...])
    o_ref[0] = _mlp_tail(x1, mod, nmlp_ref[...], w1_ref, w2_ref)


def _sg_layer(x, mod, nmix, nmlp, w_in, ln_g, ln_b, w_s, b_s, w_out, w1, w2):
    bsz, seq, _ = x.shape
    tok = pl.BlockSpec((1, TM, D), lambda b, s: (b, s, 0))
    bs_x = jnp.broadcast_to(b_s[:, :, None], (SG_GROUPS, SG_CHUNK, D // SG_GROUPS))
    return pl.pallas_call(
        _sg_kernel,
        grid=(bsz, seq // TM),
        in_specs=[
            tok,
            pl.BlockSpec((1, 6, D), lambda b, s: (b, 0, 0)),
            _const_spec((1, D)), _const_spec((1, D)),
            _const_spec((D, 2 * D)), _const_spec((1, D)), _const_spec((1, D)),
            _const_spec((SG_GROUPS, SG_CHUNK, SG_CHUNK)),
            _const_spec((SG_GROUPS, SG_CHUNK, D // SG_GROUPS)),
            _const_spec((D, D)), _const_spec((D, D_FF)), _const_spec((D_FF, D)),
        ],
        out_specs=tok,
        out_shape=jax.ShapeDtypeStruct(x.shape, F32),
        scratch_shapes=[pltpu.VMEM((TM, D), BF16)],
        compiler_params=_params(),
        name="sg_layer",
    )(x, mod, nmix.reshape(1, D), nmlp.reshape(1, D), w_in.astype(BF16),
      ln_g.reshape(1, D), ln_b.reshape(1, D), w_s, bs_x, w_out.astype(BF16),
      w1.astype(BF16), w2.astype(BF16))


def _cv_kernel(x_ref, mod_ref, nmix_ref, nmlp_ref, wpw1_ref, bpw1_ref, wdw_ref, bdw_ref,
               lng_ref, lnb_ref, wpw2_ref, bpw2_ref, w1_ref, w2_ref, o_ref, ybuf_ref, conv_ref):
    x = x_ref[0]
    mod = mod_ref[0]
    h = _rms_mod(x, nmix_ref[...], mod[1:2], mod[0:1]).astype(BF16)
    yz = _dot(h, wpw1_ref[...]) + bpw1_ref[...]
    y = yz[:, :D] * _sigmoid(yz[:, D:])

    @pl.when(pl.program_id(1) == 0)
    def _():
        ybuf_ref[0:HALO, :] = jnp.zeros((HALO, D), F32)

    ybuf_ref[HALO:HALO + TM, :] = y

    first = HALO - (CONV_W - 1)
    n_a = (CONV_W + 7) // 8
    rb = 128
    for r in range(TM // rb):
        for c in range(D // LANES):
            cols = slice(c * LANES, (c + 1) * LANES)
            win = ybuf_ref[r * rb:r * rb + rb + HALO, cols]
            acc = jnp.zeros((rb, LANES), F32) + bdw_ref[:, cols]
            for b in range(8):
                taps = [a for a in range(n_a) if 8 * a + b < CONV_W]
                zb = win[first + b:first + b + rb + 8 * taps[-1], :]
                for a in taps:
                    j = 8 * a + b
                    acc = acc + wdw_ref[j:j + 1, cols] * zb[8 * a:8 * a + rb, :]
            conv_ref[r * rb:(r + 1) * rb, cols] = acc

    ybuf_ref[0:HALO, :] = ybuf_ref[TM:TM + HALO, :]

    z = _layer_norm(conv_ref[...], lng_ref[...], lnb_ref[...])
    z = (z * _sigmoid(z)).astype(BF16)
    x1 = x + mod[2:3] * (_dot(z, wpw2_ref[...]) + bpw2_ref[...])
    o_ref[0] = _mlp_tail(x1, mod, nmlp_ref[...], w1_ref, w2_ref)


def _cv_layer(x, mod, nmix, nmlp, w_pw1, b_pw1, w_dw, b_dw, ln_g, ln_b, w_pw2, b_pw2, w1, w2):
    bsz, seq, _ = x.shape
    tok = pl.BlockSpec((1, TM, D), lambda b, s: (b, s, 0))
    wdw = jnp.zeros((HALO, D), F32).at[:CONV_W].set(w_dw)
    return pl.pallas_call(
        _cv_kernel,
        grid=(bsz, seq // TM),
        in_specs=[
            tok,
            pl.BlockSpec((1, 6, D), lambda b, s: (b, 0, 0)),
            _const_spec((1, D)), _const_spec((1, D)),
            _const_spec((D, 2 * D)), _const_spec((1, 2 * D)),
            _const_spec((HALO, D)), _const_spec((1, D)),
            _const_spec((1, D)), _const_spec((1, D)),
            _const_spec((D, D)), _const_spec((1, D)),
            _const_spec((D, D_FF)), _const_spec((D_FF, D)),
        ],
        out_specs=tok,
        out_shape=jax.ShapeDtypeStruct(x.shape, F32),
        scratch_shapes=[pltpu.VMEM((TM + HALO, D), F32), pltpu.VMEM((TM, D), F32)],
        compiler_params=_params(),
        name="cv_layer",
    )(x, mod, nmix.reshape(1, D), nmlp.reshape(1, D), w_pw1.astype(BF16), b_pw1.reshape(1, 2 * D),
      wdw, b_dw.reshape(1, D), ln_g.reshape(1, D), ln_b.reshape(1, D),
      w_pw2.astype(BF16), b_pw2.reshape(1, D), w1.astype(BF16), w2.astype(BF16))


def _fox_proj_kernel(x_ref, mod_ref, nmix_ref, wqv_ref, wk_ref, wf_ref, bf_ref, gqk_ref, gsum_ref,
                     tri_ref, sel_ref, qT_ref, vT_ref, k2_ref, aug_ref, carry_ref):
    x = x_ref[0]
    mod = mod_ref[0]
    h = _rms_mod(x, nmix_ref[...], mod[1:2], mod[0:1]).astype(BF16)

    qvT = lax.dot_general(wqv_ref[...], h, (((1,), (1,)), ((), ())), preferred_element_type=F32)
    qT = qvT[:D].reshape(N_HEADS, DH, TM)
    qT = qT * lax.rsqrt(jnp.mean(qT * qT, axis=1, keepdims=True) + EPS)
    qT = qT.reshape(D, TM).astype(BF16)
    vT = qvT[D:].astype(BF16)
    for j in range(TM // TQ):
        qT_ref[0, j] = qT[:, j * TQ:(j + 1) * TQ]
    for j in range(TM // TK):
        vT_ref[0, j] = vT[:, j * TK:(j + 1) * TK]

    k = _dot(h, wk_ref[...])
    kk = (k * k).astype(BF16)
    ss = jnp.concatenate(
        [_dot(kk[:, c * MXU:(c + 1) * MXU], gsum_ref[...]) for c in range(D // MXU)], axis=1)
    k2_ref[0] = (k * lax.rsqrt(ss * (1.0 / DH) + EPS) * gqk_ref[...]).astype(BF16)

    f_pre = _dot(h, wf_ref[...]) + bf_ref[...]
    log_f = jnp.minimum(f_pre, 0.0) - jnp.log1p(jnp.exp(-jnp.abs(f_pre)))

    @pl.when(pl.program_id(1) == 0)
    def _():
        carry_ref[...] = jnp.zeros(carry_ref.shape, F32)

    cum = carry_ref[0:1, :]
    for p in _split_bf16(log_f):
        cum = cum + _dot(tri_ref[...], p)
    carry_ref[0:1, :] = cum[TM - 1:TM, :]

    aug = jnp.zeros((TM, D), F32)
    for i, p in enumerate(_split_bf16(cum * -LOG2E)):
        aug = aug + _dot(p, sel_ref[i])
    aug_ref[0] = aug.astype(BF16)


def _fox_consts():
    g = np.kron(np.eye(MXU // DH), np.ones((DH, DH)))
    tri = np.tril(np.ones((TM, TM)))
    sel = np.zeros((N_SPLIT, LANES, D))
    for i in range(N_SPLIT):
        for hd in range(N_HEADS):
            sel[i, hd, LANES * (hd // 2) + N_SPLIT * (hd % 2) + i] = 1.0
    return (jnp.asarray(g, BF16), jnp.asarray(tri, BF16), jnp.asarray(sel, BF16))


def _fox_proj(x, mod, nmix, w_in, b_f, q_g, k_g):
    bsz, seq, _ = x.shape
    gsum, tri, sel = _fox_consts()
    wqv = jnp.concatenate([w_in[:, :D].T, w_in[:, 2 * D:3 * D].T], axis=0).astype(BF16)
    wk = w_in[:, D:2 * D].astype(BF16)
    wf = jnp.zeros((D, LANES), F32).at[:, :N_HEADS].set(w_in[:, 3 * D:]).astype(BF16)
    bf = jnp.zeros((1, LANES), F32).at[0, :N_HEADS].set(b_f)
    gqk = jnp.tile(q_g * k_g * (LOG2E * DH ** -0.5), N_HEADS).reshape(1, D)
    tok = pl.BlockSpec((1, TM, D), lambda b, s: (b, s, 0))
    return pl.pallas_call(
        _fox_proj_kernel,
        grid=(bsz, seq // TM),
        in_specs=[
            tok,
            pl.BlockSpec((1, 6, D), lambda b, s: (b, 0, 0)),
            _const_spec((1, D)),
            _const_spec((2 * D, D)), _const_spec((D, D)), _const_spec((D, LANES)),
            _const_spec((1, LANES)), _const_spec((1, D)), _const_spec((MXU, MXU)),
            _const_spec((TM, TM)), _const_spec((N_SPLIT, LANES, D)),
        ],
        out_specs=[
            pl.BlockSpec((1, TM // TQ, D, TQ), lambda b, s: (b, s, 0, 0)),
            pl.BlockSpec((1, TM // TK, D, TK), lambda b, s: (b, s, 0, 0)),
            tok, tok,
        ],
        out_shape=[
            jax.ShapeDtypeStruct((bsz, seq // TQ, D, TQ), BF16),
            jax.ShapeDtypeStruct((bsz, seq // TK, D, TK), BF16),
            jax.ShapeDtypeStruct((bsz, seq, D), BF16),
            jax.ShapeDtypeStruct((bsz, seq, D), BF16),
        ],
        scratch_shapes=[pltpu.VMEM((8, LANES), F32)],
        compiler_params=_params(),
        name="fox_proj",
    )(x, mod, nmix.reshape(1, D), wqv, wk, wf, bf, gqk, gsum, tri, sel)


def _fox_attn_kernel(qT_ref, k_ref, aug_ref, vT_ref, o_ref, qpad_ref, sa_ref, sb_ref, m_ref, acc_ref):
    qi = pl.program_id(2)
    row = lax.broadcasted_iota(jnp.int32, (LANES, TQ), 0)
    zeros = jnp.zeros((DH, TQ), BF16)
    for hd in range(HPS):
        sub = hd % 2
        pick = (row >= N_SPLIT * sub) & (row < N_SPLIT * (sub + 1))
        ones = jnp.where(pick, 1.0, 0.0).astype(BF16)
        q = qT_ref[0, 0, hd * DH:(hd + 1) * DH, :]
        parts = [q, zeros] if sub == 0 else [zeros, q]
        qpad_ref[hd] = jnp.concatenate(parts + [ones], axis=0)
    m_ref[...] = jnp.full(m_ref.shape, -1e30, F32)
    acc_ref[...] = jnp.zeros(acc_ref.shape, F32)
    ones_rows = jnp.ones((16, TK), BF16)
    key_le_query = (lax.broadcasted_iota(jnp.int32, (TK, TQ), 0)
                    <= lax.broadcasted_iota(jnp.int32, (TK, TQ), 1))

    def scores(ki, s_ref):
        k0 = pl.multiple_of(ki * TK, TK)
        for hd in range(HPS):
            pair = slice((hd // 2) * LANES, (hd // 2 + 1) * LANES)
            kk = jnp.concatenate(
                [k_ref[0, pl.ds(k0, TK), pair], aug_ref[0, pl.ds(k0, TK), pair]], axis=1)
            s_ref[hd] = _dot(kk, qpad_ref[hd])

    def update(ki, s_ref, masked):
        for hd in range(HPS):
            sT = s_ref[hd]
            if masked:
                sT = jnp.where(key_le_query, sT, -jnp.inf)
            m_old = m_ref[hd]
            m_new = jnp.maximum(m_old, jnp.max(sT, axis=0, keepdims=True))
            p = jnp.exp2(sT - m_new).astype(BF16)
            va = jnp.concatenate([vT_ref[0, ki, hd * DH:(hd + 1) * DH, :], ones_rows], axis=0)
            acc_ref[hd] = acc_ref[hd] * jnp.exp2(m_old - m_new) + _dot(va, p)
            m_ref[hd] = m_new

    scores(0, sa_ref)

    def two_tiles(j, _):
        scores(2 * j + 1, sb_ref)
        update(2 * j, sa_ref, False)
        scores(2 * j + 2, sa_ref)
        update(2 * j + 1, sb_ref, False)
        return 0

    lax.fori_loop(0, qi // 2, two_tiles, 0)

    @pl.when(qi % 2 == 0)
    def _():
        update(qi, sa_ref, True)

    @pl.when(qi % 2 == 1)
    def _():
        scores(qi, sb_ref)
        update(qi - 1, sa_ref, False)
        update(qi, sb_ref, True)

    oT = jnp.concatenate(
        [acc_ref[hd, :DH, :] / acc_ref[hd, DH:DH + 1, :] for hd in range(HPS)], axis=0)
    o_ref[0] = oT.T.astype(BF16)


def _fox_attn(qT, k2, aug, vT):
    bsz, nq, _, _ = qT.shape
    seq = k2.shape[1]
    assert TQ == TK
    gw = HPS * DH
    return pl.pallas_call(
        _fox_attn_kernel,
        grid=(bsz, N_HEADS // HPS, nq),
        in_specs=[
            pl.BlockSpec((1, 1, gw, TQ), lambda b, g, q: (b, q, g, 0)),
            pl.BlockSpec((1, seq, gw), lambda b, g, q: (b, 0, g)),
            pl.BlockSpec((1, seq, gw), lambda b, g, q: (b, 0, g)),
            pl.BlockSpec((1, seq // TK, gw, TK), lambda b, g, q: (b, 0, g, 0)),
        ],
        out_specs=pl.BlockSpec((1, TQ, gw), lambda b, g, q: (b, q, g)),
        out_shape=jax.ShapeDtypeStruct((bsz, seq, D), BF16),
        scratch_shapes=[
            pltpu.VMEM((HPS, MXU, TQ), BF16),
            pltpu.VMEM((HPS, TK, TQ), F32),
            pltpu.VMEM((HPS, TK, TQ), F32),
            pltpu.VMEM((HPS, 1, TQ), F32),
            pltpu.VMEM((HPS, DH + 16, TQ), F32),
        ],
        compiler_params=pltpu.CompilerParams(
            dimension_semantics=("arbitrary", "arbitrary", "arbitrary"),
            vmem_limit_bytes=VMEM_LIMIT),
        name="fox_attn",
    )(qT, k2, aug, vT)


def _fox_tail_kernel(x_ref, o_ref_in, mod_ref, nmlp_ref, wout_ref, w1_ref, w2_ref, out_ref):
    x = x_ref[0]
    mod = mod_ref[0]
    x1 = x + mod[2:3] * _dot(o_ref_in[0], wout_ref[...])
    out_ref[0] = _mlp_tail(x1, mod, nmlp_ref[...], w1_ref, w2_ref)


def _fox_tail(x, o, mod, nmlp, w_out, w1, w2):
    bsz, seq, _ = x.shape
    tok = pl.BlockSpec((1, TM, D), lambda b, s: (b, s, 0))
    return pl.pallas_call(
        _fox_tail_kernel,
        grid=(bsz, seq // TM),
        in_specs=[
            tok, tok,
            pl.BlockSpec((1, 6, D), lambda b, s: (b, 0, 0)),
            _const_spec((1, D)),
            _const_spec((D, D)), _const_spec((D, D_FF)), _const_spec((D_FF, D)),
        ],
        out_specs=tok,
        out_shape=jax.ShapeDtypeStruct(x.shape, F32),
        compiler_params=_params(),
        name="fox_tail",
    )(x, o, mod, nmlp.reshape(1, D), w_out.astype(BF16), w1.astype(BF16), w2.astype(BF16))


def kernel(x, c, norm_mix, norm_mlp, w_ada, b_ada, w_mlp_in, w_mlp_out, fox_w_in, fox_b_f, fox_q_norm, fox_k_norm, fox_w_out, sg_w_in, sg_ln_g, sg_ln_b, sg_w_s, sg_b_s, sg_w_out, cv_w_pw1, cv_b_pw1, cv_w_dw, cv_b_dw, cv_ln_g, cv_ln_b, cv_w_pw2, cv_b_pw2):
    depth = w_ada.shape[0]
    assert x.shape[1] % TM == 0 and x.shape[2] == D
    mods = _ada(c, w_ada, b_ada)
    for i in range(depth):
        kind, j = i % 3, i // 3
        mod = mods[i]
        if kind == 0:
            qT, vT, k2, aug = _fox_proj(x, mod, norm_mix[i], fox_w_in[j], fox_b_f[j],
                                        fox_q_norm[j], fox_k_norm[j])
            o = _fox_attn(qT, k2, aug, vT)
            x = _fox_tail(x, o, mod, norm_mlp[i], fox_w_out[j], w_mlp_in[i], w_mlp_out[i])
        elif kind == 1:
            x = _sg_layer(x, mod, norm_mix[i], norm_mlp[i], sg_w_in[j], sg_ln_g[j], sg_ln_b[j],
                          sg_w_s[j], sg_b_s[j], sg_w_out[j], w_mlp_in[i], w_mlp_out[i])
        else:
            x = _cv_layer(x, mod, norm_mix[i], norm_mlp[i], cv_w_pw1[j], cv_b_pw1[j], cv_w_dw[j],
                          cv_b_dw[j], cv_ln_g[j], cv_ln_b[j], cv_w_pw2[j], cv_b_pw2[j],
                          w_mlp_in[i], w_mlp_out[i])
    return x
```

```python
import functools

import jax
import jax.numpy as jnp
import numpy as np
from jax import lax
from jax.experimental import pallas as pl
from jax.experimental.pallas import tpu as pltpu

F32 = jnp.float32
BF16 = jnp.bfloat16

D = 1024
N_HEADS = 16
DH = D // N_HEADS
D_FF = 4 * D
EPS = 1e-6
LOG2E = 1.4426950408889634
SG_CHUNK = 128
SG_BLOCK = 64
SG_GROUPS = 8
CONV_W = 31
HALO = 32

LANES = 128
MXU = 256
TM = 512
TQ = 256
TK = 256
HPS = 4
KT_UNROLL = 4
FF_CHUNK = 1024
N_SPLIT = 3
VMEM_LIMIT = 56 * 1024 * 1024


def _const_spec(shape):
    nd = len(shape)
    return pl.BlockSpec(shape, lambda *_: (0,) * nd, pipeline_mode=pl.Buffered(1))


def _params():
    return pltpu.CompilerParams(
        dimension_semantics=("arbitrary", "arbitrary"), vmem_limit_bytes=VMEM_LIMIT)


def _dot(a, b):
    return jnp.dot(a, b, preferred_element_type=F32)


def _sigmoid(x):
    return 1.0 / (1.0 + jnp.exp(-x))


def _rms_mod(x, gain, scale, shift):
    ms = jnp.mean(x * x, axis=-1, keepdims=True)
    return (x * lax.rsqrt(ms + EPS) * gain) * (1.0 + scale) + shift


def _layer_norm(x, g, b):
    mu = jnp.mean(x, axis=-1, keepdims=True)
    xc = x - mu
    var = jnp.mean(xc * xc, axis=-1, keepdims=True)
    return xc * lax.rsqrt(var + EPS) * g + b


def _split_bf16(x):
    pieces = []
    r = x
    for _ in range(N_SPLIT):
        p = r.astype(BF16)
        pieces.append(p)
        r = r - p.astype(F32)
    return pieces


def _mlp_tail(x1, mod, nmlp, w1_ref, w2_ref):
    h = _rms_mod(x1, nmlp, mod[4:5], mod[3:4]).astype(BF16)
    acc = jnp.zeros(x1.shape, F32)
    for c in range(D_FF // FF_CHUNK):
        lo = c * FF_CHUNK
        hid = _dot(h, w1_ref[:, lo:lo + FF_CHUNK])
        hid = jnp.square(jnp.maximum(hid, 0.0)).astype(BF16)
        acc = acc + _dot(hid, w2_ref[lo:lo + FF_CHUNK, :])
    return x1 + mod[5:6] * acc


def _ada_kernel(c_ref, w_ref, b_ref, o_ref):
    c = c_ref[...]
    ca = c * _sigmoid(c)
    o_ref[0] = jnp.dot(ca, w_ref[0], preferred_element_type=F32,
                       precision=lax.Precision.HIGHEST) + b_ref[0]


def _ada(c, w_ada, b_ada):
    depth, _, n = w_ada.shape
    bsz = c.shape[0]
    rows = 8
    tn = 1536
    cp = jnp.zeros((rows, D), F32).at[:bsz].set(c)
    out = pl.pallas_call(
        _ada_kernel,
        grid=(depth, n // tn),
        in_specs=[
            pl.BlockSpec((rows, D), lambda i, j: (0, 0)),
            pl.BlockSpec((1, D, tn), lambda i, j: (i, 0, j)),
            pl.BlockSpec((1, 1, tn), lambda i, j: (i, 0, j)),
        ],
        out_specs=pl.BlockSpec((1, rows, tn), lambda i, j: (i, 0, j)),
        out_shape=jax.ShapeDtypeStruct((depth, rows, n), F32),
        compiler_params=_params(),
        name="ada_mod",
    )(cp, w_ada, b_ada.reshape(depth, 1, n))
    return out[:, :bsz].reshape(depth, bsz, 6, D)


def _sg_kernel(x_ref, mod_ref, nmix_ref, nmlp_ref, win_ref, lng_ref, lnb_ref, ws_ref, bs_ref,
               wout_ref, w1_ref, w2_ref, o_ref, gated_ref):
    x = x_ref[0]
    mod = mod_ref[0]
    h = _rms_mod(x, nmix_ref[...], mod[1:2], mod[0:1]).astype(BF16)
    uv = jax.nn.gelu(_dot(h, win_ref[...]), approximate=True)
    u = uv[:, :D]
    v = _layer_norm(uv[:, D:], lng_ref[...], lnb_ref[...]).astype(BF16)

    t = lax.broadcasted_iota(jnp.int32, (SG_CHUNK, SG_CHUNK), 0) // SG_BLOCK
    s = lax.broadcasted_iota(jnp.int32, (SG_CHUNK, SG_CHUNK), 1) // SG_BLOCK
    causal = s <= t
    gd = D // SG_GROUPS
    for g in range(SG_GROUPS):
        ws = jnp.where(causal, ws_ref[g], 0.0).astype(BF16)
        bias = bs_ref[g]
        cols = slice(g * gd, (g + 1) * gd)
        for j in range(TM // (2 * SG_CHUNK)):
            r0 = j * 2 * SG_CHUNK
            r1 = r0 + SG_CHUNK
            r2 = r1 + SG_CHUNK
            rhs = jnp.concatenate([v[r0:r1, cols], v[r1:r2, cols]], axis=1)
            mix = _dot(ws, rhs)
            gated_ref[r0:r1, cols] = (u[r0:r1, cols] * (mix[:, :gd] + bias)).astype(BF16)
            gated_ref[r1:r2, cols] = (u[r1:r2, cols] * (mix[:, gd:] + bias)).astype(BF16)

    x1 = x + mod[2:3] * _dot(gated_ref[...], wout_ref[...])
    o_ref[0] = _mlp_tail(x1, mod, nmlp_ref[...], w1_ref, w2_ref)


def _sg_layer(x, mod, nmix, nmlp, w_in, ln_g, ln_b, w_s, b_s, w_out, w1, w2):
    bsz, seq, _ = x.shape
    tok = pl.BlockSpec((1, TM, D), lambda b, s: (b, s, 0))
    bs_x = jnp.broadcast_to(b_s[:, :, None], (SG_GROUPS, SG_CHUNK, D // SG_GROUPS))
    return pl.pallas_call(
        _sg_kernel,
        grid=(bsz, seq // TM),
        in_specs=[
            tok,
            pl.BlockSpec((1, 6, D), lambda b, s: (b, 0, 0)),
            _const_spec((1, D)), _const_spec((1, D)),
            _const_spec((D, 2 * D)), _const_spec((1, D)), _const_spec((1, D)),
            _const_spec((SG_GROUPS, SG_CHUNK, SG_CHUNK)),
            _const_spec((SG_GROUPS, SG_CHUNK, D // SG_GROUPS)),
            _const_spec((D, D)), _const_spec((D, D_FF)), _const_spec((D_FF, D)),
        ],
        out_specs=tok,
        out_shape=jax.ShapeDtypeStruct(x.shape, F32),
        scratch_shapes=[pltpu.VMEM((TM, D), BF16)],
        compiler_params=_params(),
        name="sg_layer",
    )(x, mod, nmix.reshape(1, D), nmlp.reshape(1, D), w_in.astype(BF16),
      ln_g.reshape(1, D), ln_b.reshape(1, D), w_s, bs_x, w_out.astype(BF16),
      w1.astype(BF16), w2.astype(BF16))


def _cv_kernel(n_s, x_ref, moda_ref, modb_ref, nmix_ref, nmlp_ref, wpw1_ref, bpw1_ref, wdw_ref,
               bdw_ref, lng_ref, lnb_ref, wpw2_ref, bpw2_ref, w1_ref, w2_ref, o_ref,
               ybuf_ref, conv_ref, z_ref, xprev_ref, h2_ref, acc_ref):
    i = pl.program_id(0)
    n_lt = D // LANES

    @pl.when(i == 0)
    def _():
        z_ref[...] = jnp.zeros(z_ref.shape, BF16)
        xprev_ref[...] = jnp.zeros(xprev_ref.shape, F32)

    @pl.when(i % n_s == 0)
    def _():
        ybuf_ref[:, 0:HALO, :] = jnp.zeros((n_lt, HALO, LANES), F32)

    modb = modb_ref[0]
    x1 = xprev_ref[...] + modb[2:3] * (_dot(z_ref[...], wpw2_ref[...]) + bpw2_ref[...])
    h2_ref[...] = _rms_mod(x1, nmlp_ref[...], modb[4:5], modb[3:4]).astype(BF16)
    acc_ref[...] = x1

    x = x_ref[0]
    moda = moda_ref[0]
    h = _rms_mod(x, nmix_ref[...], moda[1:2], moda[0:1]).astype(BF16)
    yz = _dot(h, wpw1_ref[...]) + bpw1_ref[...]
    y = yz[:, :D] * _sigmoid(yz[:, D:])
    for c in range(n_lt):
        ybuf_ref[c, HALO:HALO + TM, :] = y[:, c * LANES:(c + 1) * LANES]
    xprev_ref[...] = x

    first = HALO - (CONV_W - 1)
    rb = 128
    win_rows = rb + HALO
    g_f = modb[5:6]

    def conv_block(r, c):
        win = ybuf_ref[c, r * rb:r * rb + win_rows, :]
        acc = jnp.zeros((rb, LANES), F32) + bdw_ref[c]
        for b in range(8):
            phase = win if b == 0 else pltpu.roll(win, win_rows - b, axis=0)
            for j in range(CONV_W):
                off = first + j
                if off % 8 == b:
                    acc = acc + wdw_ref[c, j:j + 1, :] * phase[off - b:off - b + rb, :]
        conv_ref[c, r * rb:(r + 1) * rb, :] = acc

    def chunk(c, _):
        hid = _dot(h2_ref[...], w1_ref[c])
        hid = jnp.square(jnp.maximum(hid, 0.0)).astype(BF16)
        acc_ref[...] += g_f * _dot(hid, w2_ref[c])
        for r in range(TM // rb):
            conv_block(r, c)
        return 0

    lax.fori_loop(0, n_lt, chunk, 0)
    o_ref[0] = acc_ref[...]

    ybuf_ref[:, 0:HALO, :] = ybuf_ref[:, TM:TM + HALO, :]
    conv = jnp.concatenate([conv_ref[c] for c in range(n_lt)], axis=1)
    z = _layer_norm(conv, lng_ref[...], lnb_ref[...])
    z_ref[...] = (z * _sigmoid(z)).astype(BF16)


def _cv_layer(x, mod, nmix, nmlp, w_pw1, b_pw1, w_dw, b_dw, ln_g, ln_b, w_pw2, b_pw2, w1, w2):
    bsz, seq, _ = x.shape
    n_s = seq // TM
    n = bsz * n_s
    n_lt = D // LANES
    ffc = D_FF // n_lt

    def cur(i):
        return jnp.minimum(i, n - 1)

    def prev(i):
        return jnp.maximum(i - 1, 0)

    wdw = jnp.zeros((HALO, D), F32).at[:CONV_W].set(w_dw).reshape(HALO, n_lt, LANES).transpose(1, 0, 2)
    bdw = b_dw.reshape(n_lt, 1, LANES)
    w1c = w1.astype(BF16).reshape(D, n_lt, ffc).transpose(1, 0, 2)
    w2c = w2.astype(BF16).reshape(n_lt, ffc, D)
    return pl.pallas_call(
        functools.partial(_cv_kernel, n_s),
        grid=(n + 1,),
        in_specs=[
            pl.BlockSpec((1, TM, D), lambda i: (cur(i) // n_s, cur(i) % n_s, 0)),
            pl.BlockSpec((1, 6, D), lambda i: (cur(i) // n_s, 0, 0)),
            pl.BlockSpec((1, 6, D), lambda i: (prev(i) // n_s, 0, 0)),
            _const_spec((1, D)), _const_spec((1, D)),
            _const_spec((D, 2 * D)), _const_spec((1, 2 * D)),
            _const_spec((n_lt, HALO, LANES)), _const_spec((n_lt, 1, LANES)),
            _const_spec((1, D)), _const_spec((1, D)),
            _const_spec((D, D)), _const_spec((1, D)),
            _const_spec((n_lt, D, ffc)), _const_spec((n_lt, ffc, D)),
        ],
        out_specs=pl.BlockSpec((1, TM, D), lambda i: (prev(i) // n_s, prev(i) % n_s, 0)),
        out_shape=jax.ShapeDtypeStruct(x.shape, F32),
        scratch_shapes=[
            pltpu.VMEM((n_lt, TM + HALO, LANES), F32),
            pltpu.VMEM((n_lt, TM, LANES), F32),
            pltpu.VMEM((TM, D), BF16),
            pltpu.VMEM((TM, D), F32),
            pltpu.VMEM((TM, D), BF16),
            pltpu.VMEM((TM, D), F32),
        ],
        compiler_params=pltpu.CompilerParams(
            dimension_semantics=("arbitrary",), vmem_limit_bytes=VMEM_LIMIT),
        name="cv_layer",
    )(x, mod, mod, nmix.reshape(1, D), nmlp.reshape(1, D), w_pw1.astype(BF16),
      b_pw1.reshape(1, 2 * D), wdw, bdw, ln_g.reshape(1, D), ln_b.reshape(1, D),
      w_pw2.astype(BF16), b_pw2.reshape(1, D), w1c, w2c)


def _fox_proj_kernel(x_ref, mod_ref, nmix_ref, wqv_ref, wk_ref, wf_ref, bf_ref, gqk_ref, gsum_ref,
                     tri_ref, qT_ref, vT_ref, k2_ref, aug_ref, carry_ref):
    x = x_ref[0]
    mod = mod_ref[0]
    h = _rms_mod(x, nmix_ref[...], mod[1:2], mod[0:1]).astype(BF16)

    qvT = lax.dot_general(wqv_ref[...], h, (((1,), (1,)), ((), ())), preferred_element_type=F32)
    qT = qvT[:D].reshape(N_HEADS, DH, TM)
    qT = qT * lax.rsqrt(jnp.mean(qT * qT, axis=1, keepdims=True) + EPS)
    qT = qT.reshape(D, TM).astype(BF16)
    vT = qvT[D:].astype(BF16)
    for j in range(TM // TQ):
        qT_ref[0, j] = qT[:, j * TQ:(j + 1) * TQ]
    for j in range(TM // TK):
        vT_ref[0, j] = vT[:, j * TK:(j + 1) * TK]

    k = _dot(h, wk_ref[...])
    kk = (k * k).astype(BF16)
    ss = jnp.concatenate(
        [_dot(kk[:, c * MXU:(c + 1) * MXU], gsum_ref[...]) for c in range(D // MXU)], axis=1)
    k2_ref[0] = (k * lax.rsqrt(ss * (1.0 / DH) + EPS) * gqk_ref[...]).astype(BF16)

    f_pre = _dot(h, wf_ref[...]) + bf_ref[...]
    log_f = jnp.minimum(f_pre, 0.0) - jnp.log1p(jnp.exp(-jnp.abs(f_pre)))

    @pl.when(pl.program_id(1) == 0)
    def _():
        carry_ref[...] = jnp.zeros(carry_ref.shape, F32)

    cum = carry_ref[0:1, :]
    for p in _split_bf16(log_f):
        cum = cum + _dot(tri_ref[...], p)
    carry_ref[0:1, :] = cum[TM - 1:TM, :]

    lane = lax.broadcasted_iota(jnp.int32, (TM, LANES), 1)
    aug = jnp.zeros((TM, LANES), BF16)
    for i, p in enumerate(_split_bf16(cum * -LOG2E)):
        aug = jnp.where((lane >= N_HEADS * i) & (lane < N_HEADS * (i + 1)), p, aug)
    aug_ref[0] = aug


def _fox_consts():
    g = np.kron(np.eye(MXU // DH), np.ones((DH, DH)))
    tri = np.tril(np.ones((TM, TM)))
    return jnp.asarray(g, BF16), jnp.asarray(tri, BF16)


def _fox_proj(x, mod, nmix, w_in, b_f, q_g, k_g):
    bsz, seq, _ = x.shape
    gsum, tri = _fox_consts()
    wqv = jnp.concatenate([w_in[:, :D].T, w_in[:, 2 * D:3 * D].T], axis=0).astype(BF16)
    wk = w_in[:, D:2 * D].astype(BF16)
    n_f = N_SPLIT * N_HEADS
    wf = jnp.zeros((D, LANES), F32).at[:, :n_f].set(jnp.tile(w_in[:, 3 * D:], (1, N_SPLIT))).astype(BF16)
    bf = jnp.zeros((1, LANES), F32).at[0, :n_f].set(jnp.tile(b_f, N_SPLIT))
    gqk = jnp.tile(q_g * k_g * (LOG2E * DH ** -0.5), N_HEADS).reshape(1, D)
    tok = pl.BlockSpec((1, TM, D), lambda b, s: (b, s, 0))
    return pl.pallas_call(
        _fox_proj_kernel,
        grid=(bsz, seq // TM),
        in_specs=[
            tok,
            pl.BlockSpec((1, 6, D), lambda b, s: (b, 0, 0)),
            _const_spec((1, D)),
            _const_spec((2 * D, D)), _const_spec((D, D)), _const_spec((D, LANES)),
            _const_spec((1, LANES)), _const_spec((1, D)), _const_spec((MXU, MXU)),
            _const_spec((TM, TM)),
        ],
        out_specs=[
            pl.BlockSpec((1, TM // TQ, D, TQ), lambda b, s: (b, s, 0, 0)),
            pl.BlockSpec((1, TM // TK, D, TK), lambda b, s: (b, s, 0, 0)),
            tok,
            pl.BlockSpec((1, TM, LANES), lambda b, s: (b, s, 0)),
        ],
        out_shape=[
            jax.ShapeDtypeStruct((bsz, seq // TQ, D, TQ), BF16),
            jax.ShapeDtypeStruct((bsz, seq // TK, D, TK), BF16),
            jax.ShapeDtypeStruct((bsz, seq, D), BF16),
            jax.ShapeDtypeStruct((bsz, seq, LANES), BF16),
        ],
        scratch_shapes=[pltpu.VMEM((8, LANES), F32)],
        compiler_params=_params(),
        name="fox_proj",
    )(x, mod, nmix.reshape(1, D), wqv, wk, wf, bf, gqk, gsum, tri)


def _fox_attn_kernel(qT_ref, k_ref, aug_ref, vT_ref, o_ref, qpad_ref, sa_ref, sb_ref, m_ref, acc_ref):
    head0 = pl.program_id(1) * HPS
    row = lax.broadcasted_iota(jnp.int32, (LANES, TQ), 0)
    zeros = jnp.zeros((DH, TQ), BF16)
    ones_rows = jnp.ones((16, TK), BF16)
    key_le_query = (lax.broadcasted_iota(jnp.int32, (TK, TQ), 0)
                    <= lax.broadcasted_iota(jnp.int32, (TK, TQ), 1))

    def start(qi):
        for hd in range(HPS):
            pick = row == head0 + hd
            for i in range(1, N_SPLIT):
                pick = pick | (row == head0 + hd + N_HEADS * i)
            ones = jnp.where(pick, 1.0, 0.0).astype(BF16)
            q = qT_ref[0, qi, hd * DH:(hd + 1) * DH, :]
            parts = [q, zeros] if hd % 2 == 0 else [zeros, q]
            qpad_ref[hd] = jnp.concatenate(parts + [ones], axis=0)
        m_ref[...] = jnp.full(m_ref.shape, -1e30, F32)
        acc_ref[...] = jnp.zeros(acc_ref.shape, F32)

    def scores(t, s_ref):
        k0 = pl.multiple_of(t * TK, TK)
        aug = aug_ref[0, pl.ds(k0, TK), :]
        for hd in range(HPS):
            pair = slice((hd // 2) * LANES, (hd // 2 + 1) * LANES)
            kk = jnp.concatenate([k_ref[0, pl.ds(k0, TK), pair], aug], axis=1)
            s_ref[hd] = _dot(kk, qpad_ref[hd])

    def update(t, s_ref, masked):
        for hd in range(HPS):
            sT = s_ref[hd]
            if masked:
                sT = jnp.where(key_le_query, sT, -jnp.inf)
            m_old = m_ref[hd]
            m_new = jnp.maximum(m_old, jnp.max(sT, axis=0, keepdims=True))
            p = jnp.exp2(sT - m_new).astype(BF16)
            va = jnp.concatenate([vT_ref[0, t, hd * DH:(hd + 1) * DH, :], ones_rows], axis=0)
            acc_ref[hd] = acc_ref[hd] * jnp.exp2(m_old - m_new) + _dot(va, p)
            m_ref[hd] = m_new

    def tiles(base, count):
        for i in range(0, count, 2):
            scores(base + i + 1, sb_ref)
            update(base + i, sa_ref, False)
            scores(base + i + 2, sa_ref)
            update(base + i + 1, sb_ref, False)

    def query_tile(qi, _):
        start(qi)
        scores(0, sa_ref)
        n_main = qi // KT_UNROLL

        def main(j, _):
            tiles(j * KT_UNROLL, KT_UNROLL)
            return 0

        lax.fori_loop(0, n_main, main, 0)
        left = qi - n_main * KT_UNROLL

        def pair(j, _):
            tiles(n_main * KT_UNROLL + 2 * j, 2)
            return 0

        lax.fori_loop(0, left // 2, pair, 0)

        @pl.when(qi % 2 == 0)
        def _():
            update(qi, sa_ref, True)

        @pl.when(qi % 2 == 1)
        def _():
            scores(qi, sb_ref)
            update(qi - 1, sa_ref, False)
            update(qi, sb_ref, True)

        oT = jnp.concatenate(
            [acc_ref[hd, :DH, :] / acc_ref[hd, DH:DH + 1, :] for hd in range(HPS)], axis=0)
        o_ref[0, pl.ds(pl.multiple_of(qi * TQ, TQ), TQ), :] = oT.T.astype(BF16)
        return 0

    lax.fori_loop(0, qT_ref.shape[1], query_tile, 0)


def _fox_attn(qT, k2, aug, vT):
    bsz, nq, _, _ = qT.shape
    seq = k2.shape[1]
    assert TQ == TK
    gw = HPS * DH
    return pl.pallas_call(
        _fox_attn_kernel,
        grid=(bsz, N_HEADS // HPS),
        in_specs=[
            pl.BlockSpec((1, nq, gw, TQ), lambda b, g: (b, 0, g, 0)),
            pl.BlockSpec((1, seq, gw), lambda b, g: (b, 0, g)),
            pl.BlockSpec((1, seq, LANES), lambda b, g: (b, 0, 0)),
            pl.BlockSpec((1, seq // TK, gw, TK), lambda b, g: (b, 0, g, 0)),
        ],
        out_specs=pl.BlockSpec((1, seq, gw), lambda b, g: (b, 0, g)),
        out_shape=jax.ShapeDtypeStruct((bsz, seq, D), BF16),
        scratch_shapes=[
            pltpu.VMEM((HPS, MXU, TQ), BF16),
            pltpu.VMEM((HPS, TK, TQ), F32),
            pltpu.VMEM((HPS, TK, TQ), F32),
            pltpu.VMEM((HPS, 1, TQ), F32),
            pltpu.VMEM((HPS, DH + 16, TQ), F32),
        ],
        compiler_params=_params(),
        name="fox_attn",
    )(qT, k2, aug, vT)


def _fox_tail_kernel(x_ref, o_ref_in, mod_ref, nmlp_ref, wout_ref, w1_ref, w2_ref, out_ref):
    x = x_ref[0]
    mod = mod_ref[0]
    x1 = x + mod[2:3] * _dot(o_ref_in[0], wout_ref[...])
    out_ref[0] = _mlp_tail(x1, mod, nmlp_ref[...], w1_ref, w2_ref)


def _fox_tail(x, o, mod, nmlp, w_out, w1, w2):
    bsz, seq, _ = x.shape
    tok = pl.BlockSpec((1, TM, D), lambda b, s: (b, s, 0))
    return pl.pallas_call(
        _fox_tail_kernel,
        grid=(bsz, seq // TM),
        in_specs=[
            tok, tok,
            pl.BlockSpec((1, 6, D), lambda b, s: (b, 0, 0)),
            _const_spec((1, D)),
            _const_spec((D, D)), _const_spec((D, D_FF)), _const_spec((D_FF, D)),
        ],
        out_specs=tok,
        out_shape=jax.ShapeDtypeStruct(x.shape, F32),
        compiler_params=_params(),
        name="fox_tail",
    )(x, o, mod, nmlp.reshape(1, D), w_out.astype(BF16), w1.astype(BF16), w2.astype(BF16))


def kernel(x, c, norm_mix, norm_mlp, w_ada, b_ada, w_mlp_in, w_mlp_out, fox_w_in, fox_b_f, fox_q_norm, fox_k_norm, fox_w_out, sg_w_in, sg_ln_g, sg_ln_b, sg_w_s, sg_b_s, sg_w_out, cv_w_pw1, cv_b_pw1, cv_w_dw, cv_b_dw, cv_ln_g, cv_ln_b, cv_w_pw2, cv_b_pw2):
    depth = w_ada.shape[0]
    assert x.shape[1] % TM == 0 and x.shape[2] == D
    mods = _ada(c, w_ada, b_ada)
    for i in range(depth):
        kind, j = i % 3, i // 3
        mod = mods[i]
        if kind == 0:
            qT, vT, k2, aug = _fox_proj(x, mod, norm_mix[i], fox_w_in[j], fox_b_f[j],
                                        fox_q_norm[j], fox_k_norm[j])
            o = _fox_attn(qT, k2, aug, vT)
            x = _fox_tail(x, o, mod, norm_mlp[i], fox_w_out[j], w_mlp_in[i], w_mlp_out[i])
        elif kind == 1:
            x = _sg_layer(x, mod, norm_mix[i], norm_mlp[i], sg_w_in[j], sg_ln_g[j], sg_ln_b[j],
                          sg_w_s[j], sg_b_s[j], sg_w_out[j], w_mlp_in[i], w_mlp_out[i])
        else:
            x = _cv_layer(x, mod, norm_mix[i], norm_mlp[i], cv_w_pw1[j], cv_b_pw1[j], cv_w_dw[j],
                          cv_b_dw[j], cv_ln_g[j], cv_ln_b[j], cv_w_pw2[j], cv_b_pw2[j],
                          w_mlp_in[i], w_mlp_out[i])
    return x
```

```python
import functools

import jax
import jax.numpy as jnp
import numpy as np
from jax import lax
from jax.experimental import pallas as pl
from jax.experimental.pallas import tpu as pltpu

F32 = jnp.float32
BF16 = jnp.bfloat16

D = 1024
N_HEADS = 16
DH = D // N_HEADS
D_FF = 4 * D
EPS = 1e-6
LOG2E = 1.4426950408889634
SG_CHUNK = 128
SG_BLOCK = 64
SG_GROUPS = 8
CONV_W = 31
HALO = 32

LANES = 128
MXU = 256
TM = 512
TQ = 256
TK = 256
HPS = 4
KT_UNROLL = 4
FF_CHUNK = 1024
N_SPLIT = 3
VMEM_LIMIT = 56 * 1024 * 1024


def _const_spec(shape):
    nd = len(shape)
    return pl.BlockSpec(shape, lambda *_: (0,) * nd, pipeline_mode=pl.Buffered(1))


def _params():
    return pltpu.CompilerParams(
        dimension_semantics=("arbitrary", "arbitrary"), vmem_limit_bytes=VMEM_LIMIT)


def _dot(a, b):
    return jnp.dot(a, b, preferred_element_type=F32)


def _sigmoid(x):
    return 1.0 / (1.0 + jnp.exp(-x))


def _rms_mod(x, gain, scale, shift):
    ms = jnp.mean(x * x, axis=-1, keepdims=True)
    return (x * lax.rsqrt(ms + EPS) * gain) * (1.0 + scale) + shift


def _layer_norm(x, g, b):
    mu = jnp.mean(x, axis=-1, keepdims=True)
    xc = x - mu
    var = jnp.mean(xc * xc, axis=-1, keepdims=True)
    return xc * lax.rsqrt(var + EPS) * g + b


def _split_bf16(x):
    pieces = []
    r = x
    for _ in range(N_SPLIT):
        p = r.astype(BF16)
        pieces.append(p)
        r = r - p.astype(F32)
    return pieces


def _mlp_tail(x1, mod, nmlp, w1_ref, w2_ref):
    h = _rms_mod(x1, nmlp, mod[4:5], mod[3:4]).astype(BF16)
    acc = jnp.zeros(x1.shape, F32)
    for c in range(D_FF // FF_CHUNK):
        lo = c * FF_CHUNK
        hid = _dot(h, w1_ref[:, lo:lo + FF_CHUNK])
        hid = jnp.square(jnp.maximum(hid, 0.0)).astype(BF16)
        acc = acc + _dot(hid, w2_ref[lo:lo + FF_CHUNK, :])
    return x1 + mod[5:6] * acc


def _ada_kernel(c_ref, w_ref, b_ref, o_ref):
    c = c_ref[...]
    ca = c * _sigmoid(c)
    o_ref[0] = jnp.dot(ca, w_ref[0], preferred_element_type=F32,
                       precision=lax.Precision.HIGHEST) + b_ref[0]


def _ada(c, w_ada, b_ada):
    depth, _, n = w_ada.shape
    bsz = c.shape[0]
    rows = 8
    tn = 1536
    cp = jnp.zeros((rows, D), F32).at[:bsz].set(c)
    out = pl.pallas_call(
        _ada_kernel,
        grid=(depth, n // tn),
        in_specs=[
            pl.BlockSpec((rows, D), lambda i, j: (0, 0)),
            pl.BlockSpec((1, D, tn), lambda i, j: (i, 0, j)),
            pl.BlockSpec((1, 1, tn), lambda i, j: (i, 0, j)),
        ],
        out_specs=pl.BlockSpec((1, rows, tn), lambda i, j: (i, 0, j)),
        out_shape=jax.ShapeDtypeStruct((depth, rows, n), F32),
        compiler_params=_params(),
        name="ada_mod",
    )(cp, w_ada, b_ada.reshape(depth, 1, n))
    return out[:, :bsz].reshape(depth, bsz, 6, D)


def _sg_kernel(x_ref, mod_ref, nmix_ref, nmlp_ref, win_ref, lng_ref, lnb_ref, ws_ref, bs_ref,
               wout_ref, w1_ref, w2_ref, o_ref, gated_ref):
    x = x_ref[0]
    mod = mod_ref[0]
    h = _rms_mod(x, nmix_ref[...], mod[1:2], mod[0:1]).astype(BF16)
    uv = jax.nn.gelu(_dot(h, win_ref[...]), approximate=True)
    u = uv[:, :D]
    v = _layer_norm(uv[:, D:], lng_ref[...], lnb_ref[...]).astype(BF16)

    t = lax.broadcasted_iota(jnp.int32, (SG_CHUNK, SG_CHUNK), 0) // SG_BLOCK
    s = lax.broadcasted_iota(jnp.int32, (SG_CHUNK, SG_CHUNK), 1) // SG_BLOCK
    causal = s <= t
    gd = D // SG_GROUPS
    for g in range(SG_GROUPS):
        ws = jnp.where(causal, ws_ref[g], 0.0).astype(BF16)
        bias = bs_ref[g]
        cols = slice(g * gd, (g + 1) * gd)
        for j in range(TM // (2 * SG_CHUNK)):
            r0 = j * 2 * SG_CHUNK
            r1 = r0 + SG_CHUNK
            r2 = r1 + SG_CHUNK
            rhs = jnp.concatenate([v[r0:r1, cols], v[r1:r2, cols]], axis=1)
            mix = _dot(ws, rhs)
            gated_ref[r0:r1, cols] = (u[r0:r1, cols] * (mix[:, :gd] + bias)).astype(BF16)
            gated_ref[r1:r2, cols] = (u[r1:r2, cols] * (mix[:, gd:] + bias)).astype(BF16)

    x1 = x + mod[2:3] * _dot(gated_ref[...], wout_ref[...])
    o_ref[0] = _mlp_tail(x1, mod, nmlp_ref[...], w1_ref, w2_ref)


def _sg_layer(x, mod, nmix, nmlp, w_in, ln_g, ln_b, w_s, b_s, w_out, w1, w2):
    bsz, seq, _ = x.shape
    tok = pl.BlockSpec((1, TM, D), lambda b, s: (b, s, 0))
    bs_x = jnp.broadcast_to(b_s[:, :, None], (SG_GROUPS, SG_CHUNK, D // SG_GROUPS))
    return pl.pallas_call(
        _sg_kernel,
        grid=(bsz, seq // TM),
        in_specs=[
            tok,
            pl.BlockSpec((1, 6, D), lambda b, s: (b, 0, 0)),
            _const_spec((1, D)), _const_spec((1, D)),
            _const_spec((D, 2 * D)), _const_spec((1, D)), _const_spec((1, D)),
            _const_spec((SG_GROUPS, SG_CHUNK, SG_CHUNK)),
            _const_spec((SG_GROUPS, SG_CHUNK, D // SG_GROUPS)),
            _const_spec((D, D)), _const_spec((D, D_FF)), _const_spec((D_FF, D)),
        ],
        out_specs=tok,
        out_shape=jax.ShapeDtypeStruct(x.shape, F32),
        scratch_shapes=[pltpu.VMEM((TM, D), BF16)],
        compiler_params=_params(),
        name="sg_layer",
    )(x, mod, nmix.reshape(1, D), nmlp.reshape(1, D), w_in.astype(BF16),
      ln_g.reshape(1, D), ln_b.reshape(1, D), w_s, bs_x, w_out.astype(BF16),
      w1.astype(BF16), w2.astype(BF16))


def _cv_kernel(n_s, x_ref, moda_ref, modb_ref, nmix_ref, nmlp_ref, wpw1_ref, bpw1_ref, wdw_ref,
               bdw_ref, lng_ref, lnb_ref, wpw2_ref, bpw2_ref, w1_ref, w2_ref, o_ref,
               ybuf_ref, conv_ref, z_ref, xprev_ref, h2_ref, acc_ref):
    i = pl.program_id(0)
    n_lt = D // LANES

    @pl.when(i == 0)
    def _():
        z_ref[...] = jnp.zeros(z_ref.shape, BF16)
        xprev_ref[...] = jnp.zeros(xprev_ref.shape, F32)

    @pl.when(i % n_s == 0)
    def _():
        ybuf_ref[:, 0:HALO, :] = jnp.zeros((n_lt, HALO, LANES), F32)

    modb = modb_ref[0]
    x1 = xprev_ref[...] + modb[2:3] * (_dot(z_ref[...], wpw2_ref[...]) + bpw2_ref[...])
    h2_ref[...] = _rms_mod(x1, nmlp_ref[...], modb[4:5], modb[3:4]).astype(BF16)
    acc_ref[...] = x1

    x = x_ref[0]
    moda = moda_ref[0]
    h = _rms_mod(x, nmix_ref[...], moda[1:2], moda[0:1]).astype(BF16)
    yz = _dot(h, wpw1_ref[...]) + bpw1_ref[...]
    y = yz[:, :D] * _sigmoid(yz[:, D:])
    for c in range(n_lt):
        ybuf_ref[c, HALO:HALO + TM, :] = y[:, c * LANES:(c + 1) * LANES]
    xprev_ref[...] = x

    first = HALO - (CONV_W - 1)
    rb = 128
    win_rows = rb + HALO
    g_f = modb[5:6]

    def conv_block(r, c):
        win = ybuf_ref[c, r * rb:r * rb + win_rows, :]
        acc = jnp.zeros((rb, LANES), F32) + bdw_ref[c]
        for b in range(8):
            phase = win if b == 0 else pltpu.roll(win, win_rows - b, axis=0)
            for j in range(CONV_W):
                off = first + j
                if off % 8 == b:
                    acc = acc + wdw_ref[c, j:j + 1, :] * phase[off - b:off - b + rb, :]
        conv_ref[c, r * rb:(r + 1) * rb, :] = acc

    def chunk(c, _):
        hid = _dot(h2_ref[...], w1_ref[c])
        hid = jnp.square(jnp.maximum(hid, 0.0)).astype(BF16)
        acc_ref[...] += g_f * _dot(hid, w2_ref[c])
        for r in range(TM // rb):
            conv_block(r, c)
        return 0

    lax.fori_loop(0, n_lt, chunk, 0)
    o_ref[0] = acc_ref[...]

    ybuf_ref[:, 0:HALO, :] = ybuf_ref[:, TM:TM + HALO, :]
    conv = jnp.concatenate([conv_ref[c] for c in range(n_lt)], axis=1)
    z = _layer_norm(conv, lng_ref[...], lnb_ref[...])
    z_ref[...] = (z * _sigmoid(z)).astype(BF16)


def _cv_layer(x, mod, nmix, nmlp, w_pw1, b_pw1, w_dw, b_dw, ln_g, ln_b, w_pw2, b_pw2, w1, w2):
    bsz, seq, _ = x.shape
    n_s = seq // TM
    n = bsz * n_s
    n_lt = D // LANES
    ffc = D_FF // n_lt

    def cur(i):
        return jnp.minimum(i, n - 1)

    def prev(i):
        return jnp.maximum(i - 1, 0)

    wdw = jnp.zeros((HALO, D), F32).at[:CONV_W].set(w_dw).reshape(HALO, n_lt, LANES).transpose(1, 0, 2)
    bdw = b_dw.reshape(n_lt, 1, LANES)
    w1c = w1.astype(BF16).reshape(D, n_lt, ffc).transpose(1, 0, 2)
    w2c = w2.astype(BF16).reshape(n_lt, ffc, D)
    return pl.pallas_call(
        functools.partial(_cv_kernel, n_s),
        grid=(n + 1,),
        in_specs=[
            pl.BlockSpec((1, TM, D), lambda i: (cur(i) // n_s, cur(i) % n_s, 0)),
            pl.BlockSpec((1, 6, D), lambda i: (cur(i) // n_s, 0, 0)),
            pl.BlockSpec((1, 6, D), lambda i: (prev(i) // n_s, 0, 0)),
            _const_spec((1, D)), _const_spec((1, D)),
            _const_spec((D, 2 * D)), _const_spec((1, 2 * D)),
            _const_spec((n_lt, HALO, LANES)), _const_spec((n_lt, 1, LANES)),
            _const_spec((1, D)), _const_spec((1, D)),
            _const_spec((D, D)), _const_spec((1, D)),
            _const_spec((n_lt, D, ffc)), _const_spec((n_lt, ffc, D)),
        ],
        out_specs=pl.BlockSpec((1, TM, D), lambda i: (prev(i) // n_s, prev(i) % n_s, 0)),
        out_shape=jax.ShapeDtypeStruct(x.shape, F32),
        scratch_shapes=[
            pltpu.VMEM((n_lt, TM + HALO, LANES), F32),
            pltpu.VMEM((n_lt, TM, LANES), F32),
            pltpu.VMEM((TM, D), BF16),
            pltpu.VMEM((TM, D), F32),
            pltpu.VMEM((TM, D), BF16),
            pltpu.VMEM((TM, D), F32),
        ],
        compiler_params=pltpu.CompilerParams(
            dimension_semantics=("arbitrary",), vmem_limit_bytes=VMEM_LIMIT),
        name="cv_layer",
    )(x, mod, mod, nmix.reshape(1, D), nmlp.reshape(1, D), w_pw1.astype(BF16),
      b_pw1.reshape(1, 2 * D), wdw, bdw, ln_g.reshape(1, D), ln_b.reshape(1, D),
      w_pw2.astype(BF16), b_pw2.reshape(1, D), w1c, w2c)


def _fox_proj_kernel(x_ref, mod_ref, nmix_ref, wqv_ref, wk_ref, wf_ref, bf_ref, gqk_ref, gsum_ref,
                     tri_ref, qT_ref, vT_ref, k2_ref, aug_ref, carry_ref):
    x = x_ref[0]
    mod = mod_ref[0]
    h = _rms_mod(x, nmix_ref[...], mod[1:2], mod[0:1]).astype(BF16)

    qvT = lax.dot_general(wqv_ref[...], h, (((1,), (1,)), ((), ())), preferred_element_type=F32)
    qT = qvT[:D].reshape(N_HEADS, DH, TM)
    qT = qT * lax.rsqrt(jnp.mean(qT * qT, axis=1, keepdims=True) + EPS)
    qT = qT.reshape(D, TM).astype(BF16)
    vT = qvT[D:].astype(BF16)
    for j in range(TM // TQ):
        qT_ref[0, j] = qT[:, j * TQ:(j + 1) * TQ]
    for j in range(TM // TK):
        vT_ref[0, j] = vT[:, j * TK:(j + 1) * TK]

    k = _dot(h, wk_ref[...])
    kk = (k * k).astype(BF16)
    ss = jnp.concatenate(
        [_dot(kk[:, c * MXU:(c + 1) * MXU], gsum_ref[...]) for c in range(D // MXU)], axis=1)
    k2_ref[0] = (k * lax.rsqrt(ss * (1.0 / DH) + EPS) * gqk_ref[...]).astype(BF16)

    f_pre = _dot(h, wf_ref[...]) + bf_ref[...]
    log_f = jnp.minimum(f_pre, 0.0) - jnp.log1p(jnp.exp(-jnp.abs(f_pre)))

    @pl.when(pl.program_id(1) == 0)
    def _():
        carry_ref[...] = jnp.zeros(carry_ref.shape, F32)

    cum = carry_ref[0:1, :]
    for p in _split_bf16(log_f):
        cum = cum + _dot(tri_ref[...], p)
    carry_ref[0:1, :] = cum[TM - 1:TM, :]

    lane = lax.broadcasted_iota(jnp.int32, (TM, LANES), 1)
    aug = jnp.zeros((TM, LANES), BF16)
    for i, p in enumerate(_split_bf16(cum * -LOG2E)):
        aug = jnp.where((lane >= N_HEADS * i) & (lane < N_HEADS * (i + 1)), p, aug)
    aug_ref[0] = aug


def _fox_consts():
    g = np.kron(np.eye(MXU // DH), np.ones((DH, DH)))
    tri = np.tril(np.ones((TM, TM)))
    return jnp.asarray(g, BF16), jnp.asarray(tri, BF16)


def _fox_proj(x, mod, nmix, w_in, b_f, q_g, k_g):
    bsz, seq, _ = x.shape
    gsum, tri = _fox_consts()
    wqv = jnp.concatenate([w_in[:, :D].T, w_in[:, 2 * D:3 * D].T], axis=0).astype(BF16)
    wk = w_in[:, D:2 * D].astype(BF16)
    n_f = N_SPLIT * N_HEADS
    wf = jnp.zeros((D, LANES), F32).at[:, :n_f].set(jnp.tile(w_in[:, 3 * D:], (1, N_SPLIT))).astype(BF16)
    bf = jnp.zeros((1, LANES), F32).at[0, :n_f].set(jnp.tile(b_f, N_SPLIT))
    gqk = jnp.tile(q_g * k_g * (LOG2E * DH ** -0.5), N_HEADS).reshape(1, D)
    tok = pl.BlockSpec((1, TM, D), lambda b, s: (b, s, 0))
    return pl.pallas_call(
        _fox_proj_kernel,
        grid=(bsz, seq // TM),
        in_specs=[
            tok,
            pl.BlockSpec((1, 6, D), lambda b, s: (b, 0, 0)),
            _const_spec((1, D)),
            _const_spec((2 * D, D)), _const_spec((D, D)), _const_spec((D, LANES)),
            _const_spec((1, LANES)), _const_spec((1, D)), _const_spec((MXU, MXU)),
            _const_spec((TM, TM)),
        ],
        out_specs=[
            pl.BlockSpec((1, TM // TQ, D, TQ), lambda b, s: (b, s, 0, 0)),
            pl.BlockSpec((1, TM // TK, D, TK), lambda b, s: (b, s, 0, 0)),
            tok,
            pl.BlockSpec((1, TM, LANES), lambda b, s: (b, s, 0)),
        ],
        out_shape=[
            jax.ShapeDtypeStruct((bsz, seq // TQ, D, TQ), BF16),
            jax.ShapeDtypeStruct((bsz, seq // TK, D, TK), BF16),
            jax.ShapeDtypeStruct((bsz, seq, D), BF16),
            jax.ShapeDtypeStruct((bsz, seq, LANES), BF16),
        ],
        scratch_shapes=[pltpu.VMEM((8, LANES), F32)],
        compiler_params=_params(),
        name="fox_proj",
    )(x, mod, nmix.reshape(1, D), wqv, wk, wf, bf, gqk, gsum, tri)


def _fox_attn_kernel(qT_ref, k_ref, aug_ref, vT_ref, o_ref, qpad_ref, sa_ref, sb_ref, m_ref, acc_ref):
    head0 = pl.program_id(1) * HPS
    row = lax.broadcasted_iota(jnp.int32, (LANES, TQ), 0)
    zeros = jnp.zeros((DH, TQ), BF16)
    ones_rows = jnp.ones((16, TK), BF16)
    key_le_query = (lax.broadcasted_iota(jnp.int32, (TK, TQ), 0)
                    <= lax.broadcasted_iota(jnp.int32, (TK, TQ), 1))

    def build_qpad(qi):
        for hd in range(HPS):
            pick = row == head0 + hd
            for i in range(1, N_SPLIT):
                pick = pick | (row == head0 + hd + N_HEADS * i)
            ones = jnp.where(pick, 1.0, 0.0).astype(BF16)
            q = qT_ref[0, qi, hd * DH:(hd + 1) * DH, :]
            parts = [q, zeros] if hd % 2 == 0 else [zeros, q]
            qpad_ref[(qi % 2) * HPS + hd] = jnp.concatenate(parts + [ones], axis=0)

    def init():
        m_ref[...] = jnp.full(m_ref.shape, -1e30, F32)
        acc_ref[...] = jnp.zeros(acc_ref.shape, F32)

    def finish(qi):
        oT = jnp.concatenate(
            [acc_ref[hd, :DH, :] / acc_ref[hd, DH:DH + 1, :] for hd in range(HPS)], axis=0)
        o_ref[0, pl.ds(pl.multiple_of(qi * TQ, TQ), TQ), :] = oT.T.astype(BF16)
        init()

    def scores(t, s_ref, q):
        k0 = pl.multiple_of(t * TK, TK)
        aug = aug_ref[0, pl.ds(k0, TK), :]
        for hd in range(HPS):
            pair = slice((hd // 2) * LANES, (hd // 2 + 1) * LANES)
            kk = jnp.concatenate([k_ref[0, pl.ds(k0, TK), pair], aug], axis=1)
            s_ref[hd] = _dot(kk, qpad_ref[(q % 2) * HPS + hd])

    def update(t, s_ref, masked):
        for hd in range(HPS):
            sT = s_ref[hd]
            if masked:
                sT = jnp.where(key_le_query, sT, -jnp.inf)
            m_old = m_ref[hd]
            m_new = jnp.maximum(m_old, jnp.max(sT, axis=0, keepdims=True))
            p = jnp.exp2(sT - m_new).astype(BF16)
            va = jnp.concatenate([vT_ref[0, t, hd * DH:(hd + 1) * DH, :], ones_rows], axis=0)
            acc_ref[hd] = acc_ref[hd] * jnp.exp2(m_old - m_new) + _dot(va, p)
            m_ref[hd] = m_new

    def tiles(base, count, q):
        for i in range(0, count, 2):
            scores(base + i + 1, sb_ref, q)
            update(base + i, sa_ref, False)
            scores(base + i + 2, sa_ref, q)
            update(base + i + 1, sb_ref, False)

    n_q = qT_ref.shape[1]

    def query_tile(qi, _):
        nxt = jnp.minimum(qi + 1, n_q - 1)
        n_main = qi // KT_UNROLL

        def main(j, _):
            tiles(j * KT_UNROLL, KT_UNROLL, qi)
            return 0

        lax.fori_loop(0, n_main, main, 0)
        left = qi - n_main * KT_UNROLL

        def pair(j, _):
            tiles(n_main * KT_UNROLL + 2 * j, 2, qi)
            return 0

        lax.fori_loop(0, left // 2, pair, 0)

        @pl.when(qi % 2 == 0)
        def _():
            update(qi, sa_ref, True)
            build_qpad(nxt)
            scores(0, sa_ref, nxt)
            finish(qi)

        @pl.when(qi % 2 == 1)
        def _():
            scores(qi, sb_ref, qi)
            update(qi - 1, sa_ref, False)
            build_qpad(nxt)
            scores(0, sa_ref, nxt)
            update(qi, sb_ref, True)
            finish(qi)

        return 0

    build_qpad(0)
    init()
    scores(0, sa_ref, 0)
    lax.fori_loop(0, n_q, query_tile, 0)


def _fox_attn(qT, k2, aug, vT):
    bsz, nq, _, _ = qT.shape
    seq = k2.shape[1]
    assert TQ == TK
    gw = HPS * DH
    return pl.pallas_call(
        _fox_attn_kernel,
        grid=(bsz, N_HEADS // HPS),
        in_specs=[
            pl.BlockSpec((1, nq, gw, TQ), lambda b, g: (b, 0, g, 0)),
            pl.BlockSpec((1, seq, gw), lambda b, g: (b, 0, g)),
            pl.BlockSpec((1, seq, LANES), lambda b, g: (b, 0, 0)),
            pl.BlockSpec((1, seq // TK, gw, TK), lambda b, g: (b, 0, g, 0)),
        ],
        out_specs=pl.BlockSpec((1, seq, gw), lambda b, g: (b, 0, g)),
        out_shape=jax.ShapeDtypeStruct((bsz, seq, D), BF16),
        scratch_shapes=[
            pltpu.VMEM((2 * HPS, MXU, TQ), BF16),
            pltpu.VMEM((HPS, TK, TQ), F32),
            pltpu.VMEM((HPS, TK, TQ), F32),
            pltpu.VMEM((HPS, 1, TQ), F32),
            pltpu.VMEM((HPS, DH + 16, TQ), F32),
        ],
        compiler_params=_params(),
        name="fox_attn",
    )(qT, k2, aug, vT)


def _fox_tail_kernel(x_ref, o_ref_in, mod_ref, nmlp_ref, wout_ref, w1_ref, w2_ref, out_ref):
    x = x_ref[0]
    mod = mod_ref[0]
    x1 = x + mod[2:3] * _dot(o_ref_in[0], wout_ref[...])
    out_ref[0] = _mlp_tail(x1, mod, nmlp_ref[...], w1_ref, w2_ref)


def _fox_tail(x, o, mod, nmlp, w_out, w1, w2):
    bsz, seq, _ = x.shape
    tok = pl.BlockSpec((1, TM, D), lambda b, s: (b, s, 0))
    return pl.pallas_call(
        _fox_tail_kernel,
        grid=(bsz, seq // TM),
        in_specs=[
            tok, tok,
            pl.BlockSpec((1, 6, D), lambda b, s: (b, 0, 0)),
            _const_spec((1, D)),
            _const_spec((D, D)), _const_spec((D, D_FF)), _const_spec((D_FF, D)),
        ],
        out_specs=tok,
        out_shape=jax.ShapeDtypeStruct(x.shape, F32),
        compiler_params=_params(),
        name="fox_tail",
    )(x, o, mod, nmlp.reshape(1, D), w_out.astype(BF16), w1.astype(BF16), w2.astype(BF16))


def kernel(x, c, norm_mix, norm_mlp, w_ada, b_ada, w_mlp_in, w_mlp_out, fox_w_in, fox_b_f, fox_q_norm, fox_k_norm, fox_w_out, sg_w_in, sg_ln_g, sg_ln_b, sg_w_s, sg_b_s, sg_w_out, cv_w_pw1, cv_b_pw1, cv_w_dw, cv_b_dw, cv_ln_g, cv_ln_b, cv_w_pw2, cv_b_pw2):
    depth = w_ada.shape[0]
    assert x.shape[1] % TM == 0 and x.shape[2] == D
    mods = _ada(c, w_ada, b_ada)
    for i in range(depth):
        kind, j = i % 3, i // 3
        mod = mods[i]
        if kind == 0:
            qT, vT, k2, aug = _fox_proj(x, mod, norm_mix[i], fox_w_in[j], fox_b_f[j],
                                        fox_q_norm[j], fox_k_norm[j])
            o = _fox_attn(qT, k2, aug, vT)
            x = _fox_tail(x, o, mod, norm_mlp[i], fox_w_out[j], w_mlp_in[i], w_mlp_out[i])
        elif kind == 1:
            x = _sg_layer(x, mod, norm_mix[i], norm_mlp[i], sg_w_in[j], sg_ln_g[j], sg_ln_b[j],
                          sg_w_s[j], sg_b_s[j], sg_w_out[j], w_mlp_in[i], w_mlp_out[i])
        else:
            x = _cv_layer(x, mod, norm_mix[i], norm_mlp[i], cv_w_pw1[j], cv_b_pw1[j], cv_w_dw[j],
                          cv_b_dw[j], cv_ln_g[j], cv_ln_b[j], cv_w_pw2[j], cv_b_pw2[j],
                          w_mlp_in[i], w_mlp_out[i])
    return x
```

```python
import functools

import jax
import jax.numpy as jnp
import numpy as np
from jax import lax
from jax.experimental import pallas as pl
from jax.experimental.pallas import tpu as pltpu

F32 = jnp.float32
BF16 = jnp.bfloat16

D = 1024
N_HEADS = 16
DH = D // N_HEADS
D_FF = 4 * D
EPS = 1e-6
LOG2E = 1.4426950408889634
SG_CHUNK = 128
SG_BLOCK = 64
SG_GROUPS = 8
CONV_W = 31
HALO = 32

LANES = 128
MXU = 256
TM = 512
TQ = 256
TK = 256
HPS = 4
KT_UNROLL = 4
FF_CHUNK = 1024
N_SPLIT = 3
VMEM_LIMIT = 56 * 1024 * 1024


def _const_spec(shape):
    nd = len(shape)
    return pl.BlockSpec(shape, lambda *_: (0,) * nd, pipeline_mode=pl.Buffered(1))


def _params():
    return pltpu.CompilerParams(
        dimension_semantics=("arbitrary", "arbitrary"), vmem_limit_bytes=VMEM_LIMIT)


def _dot(a, b):
    return jnp.dot(a, b, preferred_element_type=F32)


def _sigmoid(x):
    return 1.0 / (1.0 + jnp.exp(-x))


def _rms_mod(x, gain, scale, shift):
    ms = jnp.mean(x * x, axis=-1, keepdims=True)
    return (x * lax.rsqrt(ms + EPS) * gain) * (1.0 + scale) + shift


def _layer_norm(x, g, b):
    mu = jnp.mean(x, axis=-1, keepdims=True)
    xc = x - mu
    var = jnp.mean(xc * xc, axis=-1, keepdims=True)
    return xc * lax.rsqrt(var + EPS) * g + b


def _split_bf16(x):
    pieces = []
    r = x
    for _ in range(N_SPLIT):
        p = r.astype(BF16)
        pieces.append(p)
        r = r - p.astype(F32)
    return pieces


def _mlp_tail(x1, mod, nmlp, w1_ref, w2_ref):
    h = _rms_mod(x1, nmlp, mod[4:5], mod[3:4]).astype(BF16)
    acc = jnp.zeros(x1.shape, F32)
    for c in range(D_FF // FF_CHUNK):
        lo = c * FF_CHUNK
        hid = _dot(h, w1_ref[:, lo:lo + FF_CHUNK])
        hid = jnp.square(jnp.maximum(hid, 0.0)).astype(BF16)
        acc = acc + _dot(hid, w2_ref[lo:lo + FF_CHUNK, :])
    return x1 + mod[5:6] * acc


def _ada_kernel(c_ref, w_ref, b_ref, o_ref):
    c = c_ref[...]
    rows = c.shape[0]
    ca = _split_bf16(c * _sigmoid(c))
    w = w_ref[0]
    w_hi = w.astype(BF16)
    w_lo = (w - w_hi.astype(F32)).astype(BF16)
    r_hi = _dot(jnp.concatenate(ca, axis=0), w_hi)
    r_lo = _dot(jnp.concatenate(ca[:2], axis=0), w_lo)
    out = b_ref[0] + r_lo[0:rows] + r_lo[rows:2 * rows]
    for i in range(N_SPLIT):
        out = out + r_hi[i * rows:(i + 1) * rows]
    o_ref[0] = out


def _ada(c, w_ada, b_ada):
    depth, _, n = w_ada.shape
    bsz = c.shape[0]
    rows = 16
    tn = 1536
    cp = jnp.zeros((rows, D), F32).at[:bsz].set(c)
    out = pl.pallas_call(
        _ada_kernel,
        grid=(depth, n // tn),
        in_specs=[
            pl.BlockSpec((rows, D), lambda i, j: (0, 0)),
            pl.BlockSpec((1, D, tn), lambda i, j: (i, 0, j)),
            pl.BlockSpec((1, 1, tn), lambda i, j: (i, 0, j)),
        ],
        out_specs=pl.BlockSpec((1, rows, tn), lambda i, j: (i, 0, j)),
        out_shape=jax.ShapeDtypeStruct((depth, rows, n), F32),
        compiler_params=_params(),
        name="ada_mod",
    )(cp, w_ada, b_ada.reshape(depth, 1, n))
    return out[:, :bsz].reshape(depth, bsz, 6, D)


def _sg_kernel(x_ref, mod_ref, nmix_ref, nmlp_ref, win_ref, lng_ref, lnb_ref, ws_ref, bs_ref,
               wout_ref, w1_ref, w2_ref, o_ref, gated_ref):
    x = x_ref[0]
    mod = mod_ref[0]
    h = _rms_mod(x, nmix_ref[...], mod[1:2], mod[0:1]).astype(BF16)
    uv = jax.nn.gelu(_dot(h, win_ref[...]), approximate=True)
    u = uv[:, :D]
    v = _layer_norm(uv[:, D:], lng_ref[...], lnb_ref[...]).astype(BF16)

    t = lax.broadcasted_iota(jnp.int32, (SG_CHUNK, SG_CHUNK), 0) // SG_BLOCK
    s = lax.broadcasted_iota(jnp.int32, (SG_CHUNK, SG_CHUNK), 1) // SG_BLOCK
    causal = s <= t
    gd = D // SG_GROUPS
    for g in range(SG_GROUPS):
        ws = jnp.where(causal, ws_ref[g], 0.0).astype(BF16)
        bias = bs_ref[g]
        cols = slice(g * gd, (g + 1) * gd)
        for j in range(TM // (2 * SG_CHUNK)):
            r0 = j * 2 * SG_CHUNK
            r1 = r0 + SG_CHUNK
            r2 = r1 + SG_CHUNK
            rhs = jnp.concatenate([v[r0:r1, cols], v[r1:r2, cols]], axis=1)
            mix = _dot(ws, rhs)
            gated_ref[r0:r1, cols] = (u[r0:r1, cols] * (mix[:, :gd] + bias)).astype(BF16)
            gated_ref[r1:r2, cols] = (u[r1:r2, cols] * (mix[:, gd:] + bias)).astype(BF16)

    x1 = x + mod[2:3] * _dot(gated_ref[...], wout_ref[...])
    o_ref[0] = _mlp_tail(x1, mod, nmlp_ref[...], w1_ref, w2_ref)


def _sg_layer(x, mod, nmix, nmlp, w_in, ln_g, ln_b, w_s, b_s, w_out, w1, w2):
    bsz, seq, _ = x.shape
    tok = pl.BlockSpec((1, TM, D), lambda b, s: (b, s, 0))
    bs_x = jnp.broadcast_to(b_s[:, :, None], (SG_GROUPS, SG_CHUNK, D // SG_GROUPS))
    return pl.pallas_call(
        _sg_kernel,
        grid=(bsz, seq // TM),
        in_specs=[
            tok,
            pl.BlockSpec((1, 6, D), lambda b, s: (b, 0, 0)),
            _const_spec((1, D)), _const_spec((1, D)),
            _const_spec((D, 2 * D)), _const_spec((1, D)), _const_spec((1, D)),
            _const_spec((SG_GROUPS, SG_CHUNK, SG_CHUNK)),
            _const_spec((SG_GROUPS, SG_CHUNK, D // SG_GROUPS)),
            _const_spec((D, D)), _const_spec((D, D_FF)), _const_spec((D_FF, D)),
        ],
        out_specs=tok,
        out_shape=jax.ShapeDtypeStruct(x.shape, F32),
        scratch_shapes=[pltpu.VMEM((TM, D), BF16)],
        compiler_params=_params(),
        name="sg_layer",
    )(x, mod, nmix.reshape(1, D), nmlp.reshape(1, D), w_in.astype(BF16),
      ln_g.reshape(1, D), ln_b.reshape(1, D), w_s, bs_x, w_out.astype(BF16),
      w1.astype(BF16), w2.astype(BF16))


def _cv_kernel(n_s, x_ref, moda_ref, modb_ref, nmix_ref, nmlp_ref, wpw1_ref, bpw1_ref, wdw_ref,
               bdw_ref, lng_ref, lnb_ref, wpw2_ref, bpw2_ref, w1_ref, w2_ref, o_ref,
               ybuf_ref, conv_ref, xprev_ref, x1_ref, h2_ref, acc_ref):
    i = pl.program_id(0)
    n_lt = D // LANES

    @pl.when(i == 0)
    def _():
        conv_ref[...] = jnp.zeros(conv_ref.shape, F32)
        xprev_ref[...] = jnp.zeros(xprev_ref.shape, F32)

    @pl.when(i % n_s == 0)
    def _():
        ybuf_ref[:, 0:HALO, :] = jnp.zeros((n_lt, HALO, LANES), F32)

    modb = modb_ref[0]
    conv = jnp.concatenate([conv_ref[c] for c in range(n_lt)], axis=1)
    z = _layer_norm(conv, lng_ref[...], lnb_ref[...])
    z = (z * _sigmoid(z)).astype(BF16)
    x1 = xprev_ref[...] + modb[2:3] * (_dot(z, wpw2_ref[...]) + bpw2_ref[...])
    x1_ref[...] = x1
    h2_ref[...] = _rms_mod(x1, nmlp_ref[...], modb[4:5], modb[3:4]).astype(BF16)
    acc_ref[...] = jnp.zeros(acc_ref.shape, F32)

    x = x_ref[0]
    moda = moda_ref[0]
    h = _rms_mod(x, nmix_ref[...], moda[1:2], moda[0:1]).astype(BF16)
    yz = _dot(h, wpw1_ref[...]) + bpw1_ref[...]
    y = yz[:, :D] * _sigmoid(yz[:, D:])
    for c in range(n_lt):
        ybuf_ref[c, HALO:HALO + TM, :] = y[:, c * LANES:(c + 1) * LANES]
    xprev_ref[...] = x

    first = HALO - (CONV_W - 1)
    rb = 128
    win_rows = rb + HALO

    def conv_block(r, c):
        win = ybuf_ref[c, r * rb:r * rb + win_rows, :]
        acc = jnp.zeros((rb, LANES), F32) + bdw_ref[c]
        for b in range(8):
            phase = win if b == 0 else pltpu.roll(win, win_rows - b, axis=0)
            for j in range(CONV_W):
                off = first + j
                if off % 8 == b:
                    acc = acc + wdw_ref[c, j:j + 1, :] * phase[off - b:off - b + rb, :]
        conv_ref[c, r * rb:(r + 1) * rb, :] = acc

    def chunk(c, _):
        hid = jnp.maximum(_dot(h2_ref[...], w1_ref[c]).astype(BF16), 0.0)
        acc_ref[...] += _dot(hid * hid, w2_ref[c])
        for r in range(TM // rb):
            conv_block(r, c)
        return 0

    lax.fori_loop(0, n_lt, chunk, 0)
    o_ref[0] = x1_ref[...] + modb[5:6] * acc_ref[...]
    ybuf_ref[:, 0:HALO, :] = ybuf_ref[:, TM:TM + HALO, :]


def _cv_layer(x, mod, nmix, nmlp, w_pw1, b_pw1, w_dw, b_dw, ln_g, ln_b, w_pw2, b_pw2, w1, w2):
    bsz, seq, _ = x.shape
    n_s = seq // TM
    n = bsz * n_s
    n_lt = D // LANES
    ffc = D_FF // n_lt

    def cur(i):
        return jnp.minimum(i, n - 1)

    def prev(i):
        return jnp.maximum(i - 1, 0)

    wdw = jnp.zeros((HALO, D), F32).at[:CONV_W].set(w_dw).reshape(HALO, n_lt, LANES).transpose(1, 0, 2)
    bdw = b_dw.reshape(n_lt, 1, LANES)
    w1c = w1.astype(BF16).reshape(D, n_lt, ffc).transpose(1, 0, 2)
    w2c = w2.astype(BF16).reshape(n_lt, ffc, D)
    return pl.pallas_call(
        functools.partial(_cv_kernel, n_s),
        grid=(n + 1,),
        in_specs=[
            pl.BlockSpec((1, TM, D), lambda i: (cur(i) // n_s, cur(i) % n_s, 0)),
            pl.BlockSpec((1, 6, D), lambda i: (cur(i) // n_s, 0, 0)),
            pl.BlockSpec((1, 6, D), lambda i: (prev(i) // n_s, 0, 0)),
            _const_spec((1, D)), _const_spec((1, D)),
            _const_spec((D, 2 * D)), _const_spec((1, 2 * D)),
            _const_spec((n_lt, HALO, LANES)), _const_spec((n_lt, 1, LANES)),
            _const_spec((1, D)), _const_spec((1, D)),
            _const_spec((D, D)), _const_spec((1, D)),
            _const_spec((n_lt, D, ffc)), _const_spec((n_lt, ffc, D)),
        ],
        out_specs=pl.BlockSpec((1, TM, D), lambda i: (prev(i) // n_s, prev(i) % n_s, 0)),
        out_shape=jax.ShapeDtypeStruct(x.shape, F32),
        scratch_shapes=[
            pltpu.VMEM((n_lt, TM + HALO, LANES), F32),
            pltpu.VMEM((n_lt, TM, LANES), F32),
            pltpu.VMEM((TM, D), F32),
            pltpu.VMEM((TM, D), F32),
            pltpu.VMEM((TM, D), BF16),
            pltpu.VMEM((TM, D), F32),
        ],
        compiler_params=pltpu.CompilerParams(
            dimension_semantics=("arbitrary",), vmem_limit_bytes=VMEM_LIMIT),
        name="cv_layer",
    )(x, mod, mod, nmix.reshape(1, D), nmlp.reshape(1, D), w_pw1.astype(BF16),
      b_pw1.reshape(1, 2 * D), wdw, bdw, ln_g.reshape(1, D), ln_b.reshape(1, D),
      w_pw2.astype(BF16), b_pw2.reshape(1, D), w1c, w2c)


def _fox_proj_kernel(x_ref, mod_ref, nmix_ref, wqv_ref, wk_ref, wf_ref, bf_ref, gqk_ref, gsum_ref,
                     tri_ref, qT_ref, vT_ref, k2_ref, aug_ref, carry_ref):
    x = x_ref[0]
    mod = mod_ref[0]
    h = _rms_mod(x, nmix_ref[...], mod[1:2], mod[0:1]).astype(BF16)

    qvT = lax.dot_general(wqv_ref[...], h, (((1,), (1,)), ((), ())), preferred_element_type=F32)
    qT = qvT[:D].reshape(N_HEADS, DH, TM)
    qT = qT * lax.rsqrt(jnp.mean(qT * qT, axis=1, keepdims=True) + EPS)
    qT = qT.reshape(D, TM).astype(BF16)
    vT = qvT[D:].astype(BF16)
    for j in range(TM // TQ):
        qT_ref[0, j] = qT[:, j * TQ:(j + 1) * TQ]
    for j in range(TM // TK):
        vT_ref[0, j] = vT[:, j * TK:(j + 1) * TK]

    k = _dot(h, wk_ref[...])
    kk = (k * k).astype(BF16)
    ss = jnp.concatenate(
        [_dot(kk[:, c * MXU:(c + 1) * MXU], gsum_ref[...]) for c in range(D // MXU)], axis=1)
    k2_ref[0] = (k * lax.rsqrt(ss * (1.0 / DH) + EPS) * gqk_ref[...]).astype(BF16)

    f_pre = _dot(h, wf_ref[...]) + bf_ref[...]
    log_f = jnp.minimum(f_pre, 0.0) - jnp.log1p(jnp.exp(-jnp.abs(f_pre)))

    @pl.when(pl.program_id(1) == 0)
    def _():
        carry_ref[...] = jnp.zeros(carry_ref.shape, F32)

    cum = carry_ref[0:1, :]
    for p in _split_bf16(log_f):
        cum = cum + _dot(tri_ref[...], p)
    carry_ref[0:1, :] = cum[TM - 1:TM, :]

    lane = lax.broadcasted_iota(jnp.int32, (TM, LANES), 1)
    aug = jnp.zeros((TM, LANES), BF16)
    for i, p in enumerate(_split_bf16(cum * -LOG2E)):
        aug = jnp.where((lane >= N_HEADS * i) & (lane < N_HEADS * (i + 1)), p, aug)
    aug_ref[0] = aug


def _fox_consts():
    g = np.kron(np.eye(MXU // DH), np.ones((DH, DH)))
    tri = np.tril(np.ones((TM, TM)))
    return jnp.asarray(g, BF16), jnp.asarray(tri, BF16)


def _fox_proj(x, mod, nmix, w_in, b_f, q_g, k_g):
    bsz, seq, _ = x.shape
    gsum, tri = _fox_consts()
    wqv = jnp.concatenate([w_in[:, :D].T, w_in[:, 2 * D:3 * D].T], axis=0).astype(BF16)
    wk = w_in[:, D:2 * D].astype(BF16)
    n_f = N_SPLIT * N_HEADS
    wf = jnp.zeros((D, LANES), F32).at[:, :n_f].set(jnp.tile(w_in[:, 3 * D:], (1, N_SPLIT))).astype(BF16)
    bf = jnp.zeros((1, LANES), F32).at[0, :n_f].set(jnp.tile(b_f, N_SPLIT))
    gqk = jnp.tile(q_g * k_g * (LOG2E * DH ** -0.5), N_HEADS).reshape(1, D)
    tok = pl.BlockSpec((1, TM, D), lambda b, s: (b, s, 0))
    return pl.pallas_call(
        _fox_proj_kernel,
        grid=(bsz, seq // TM),
        in_specs=[
            tok,
            pl.BlockSpec((1, 6, D), lambda b, s: (b, 0, 0)),
            _const_spec((1, D)),
            _const_spec((2 * D, D)), _const_spec((D, D)), _const_spec((D, LANES)),
            _const_spec((1, LANES)), _const_spec((1, D)), _const_spec((MXU, MXU)),
            _const_spec((TM, TM)),
        ],
        out_specs=[
            pl.BlockSpec((1, TM // TQ, D, TQ), lambda b, s: (b, s, 0, 0)),
            pl.BlockSpec((1, TM // TK, D, TK), lambda b, s: (b, s, 0, 0)),
            tok,
            pl.BlockSpec((1, TM, LANES), lambda b, s: (b, s, 0)),
        ],
        out_shape=[
            jax.ShapeDtypeStruct((bsz, seq // TQ, D, TQ), BF16),
            jax.ShapeDtypeStruct((bsz, seq // TK, D, TK), BF16),
            jax.ShapeDtypeStruct((bsz, seq, D), BF16),
            jax.ShapeDtypeStruct((bsz, seq, LANES), BF16),
        ],
        scratch_shapes=[pltpu.VMEM((8, LANES), F32)],
        compiler_params=_params(),
        name="fox_proj",
    )(x, mod, nmix.reshape(1, D), wqv, wk, wf, bf, gqk, gsum, tri)


def _fox_attn_kernel(qT_ref, k_ref, aug_ref, vT_ref, o_ref, qpad_ref, sa_ref, sb_ref, m_ref, acc_ref):
    head0 = pl.program_id(1) * HPS
    row = lax.broadcasted_iota(jnp.int32, (LANES, TQ), 0)
    zeros = jnp.zeros((DH, TQ), BF16)
    ones_rows = jnp.ones((16, TK), BF16)
    key_le_query = (lax.broadcasted_iota(jnp.int32, (TK, TQ), 0)
                    <= lax.broadcasted_iota(jnp.int32, (TK, TQ), 1))

    def build_qpad(qi):
        for hd in range(HPS):
            pick = row == head0 + hd
            for i in range(1, N_SPLIT):
                pick = pick | (row == head0 + hd + N_HEADS * i)
            ones = jnp.where(pick, 1.0, 0.0).astype(BF16)
            q = qT_ref[0, qi, hd * DH:(hd + 1) * DH, :]
            parts = [q, zeros] if hd % 2 == 0 else [zeros, q]
            qpad_ref[(qi % 2) * HPS + hd] = jnp.concatenate(parts + [ones], axis=0)

    def init():
        m_ref[...] = jnp.full(m_ref.shape, -1e30, F32)
        acc_ref[...] = jnp.zeros(acc_ref.shape, F32)

    def finish(qi):
        oT = jnp.concatenate(
            [acc_ref[hd, :DH, :] / acc_ref[hd, DH:DH + 1, :] for hd in range(HPS)], axis=0)
        o_ref[0, pl.ds(pl.multiple_of(qi * TQ, TQ), TQ), :] = oT.T.astype(BF16)
        init()

    def scores(t, s_ref, q):
        k0 = pl.multiple_of(t * TK, TK)
        aug = aug_ref[0, pl.ds(k0, TK), :]
        for hd in range(HPS):
            pair = slice((hd // 2) * LANES, (hd // 2 + 1) * LANES)
            kk = jnp.concatenate([k_ref[0, pl.ds(k0, TK), pair], aug], axis=1)
            s_ref[hd] = _dot(kk, qpad_ref[(q % 2) * HPS + hd])

    def update(t, s_ref, masked):
        for hd in range(HPS):
            sT = s_ref[hd]
            if masked:
                sT = jnp.where(key_le_query, sT, -jnp.inf)
            m_old = m_ref[hd]
            m_new = jnp.maximum(m_old, jnp.max(sT, axis=0, keepdims=True))
            p = jnp.exp2(sT - m_new).astype(BF16)
            va = jnp.concatenate([vT_ref[0, t, hd * DH:(hd + 1) * DH, :], ones_rows], axis=0)
            acc_ref[hd] = acc_ref[hd] * jnp.exp2(m_old - m_new) + _dot(va, p)
            m_ref[hd] = m_new

    def tiles(base, count, q):
        for i in range(0, count, 2):
            scores(base + i + 1, sb_ref, q)
            update(base + i, sa_ref, False)
            scores(base + i + 2, sa_ref, q)
            update(base + i + 1, sb_ref, False)

    n_q = qT_ref.shape[1]

    def query_tile(qi, _):
        nxt = jnp.minimum(qi + 1, n_q - 1)
        n_main = qi // KT_UNROLL

        def main(j, _):
            tiles(j * KT_UNROLL, KT_UNROLL, qi)
            return 0

        lax.fori_loop(0, n_main, main, 0)
        left = qi - n_main * KT_UNROLL

        def pair(j, _):
            tiles(n_main * KT_UNROLL + 2 * j, 2, qi)
            return 0

        lax.fori_loop(0, left // 2, pair, 0)

        @pl.when(qi % 2 == 0)
        def _():
            update(qi, sa_ref, True)
            build_qpad(nxt)
            scores(0, sa_ref, nxt)
            finish(qi)

        @pl.when(qi % 2 == 1)
        def _():
            scores(qi, sb_ref, qi)
            update(qi - 1, sa_ref, False)
            build_qpad(nxt)
            scores(0, sa_ref, nxt)
            update(qi, sb_ref, True)
            finish(qi)

        return 0

    build_qpad(0)
    init()
    scores(0, sa_ref, 0)
    lax.fori_loop(0, n_q, query_tile, 0)


def _fox_attn(qT, k2, aug, vT):
    bsz, nq, _, _ = qT.shape
    seq = k2.shape[1]
    assert TQ == TK
    gw = HPS * DH
    return pl.pallas_call(
        _fox_attn_kernel,
        grid=(bsz, N_HEADS // HPS),
        in_specs=[
            pl.BlockSpec((1, nq, gw, TQ), lambda b, g: (b, 0, g, 0)),
            pl.BlockSpec((1, seq, gw), lambda b, g: (b, 0, g)),
            pl.BlockSpec((1, seq, LANES), lambda b, g: (b, 0, 0)),
            pl.BlockSpec((1, seq // TK, gw, TK), lambda b, g: (b, 0, g, 0)),
        ],
        out_specs=pl.BlockSpec((1, seq, gw), lambda b, g: (b, 0, g)),
        out_shape=jax.ShapeDtypeStruct((bsz, seq, D), BF16),
        scratch_shapes=[
            pltpu.VMEM((2 * HPS, MXU, TQ), BF16),
            pltpu.VMEM((HPS, TK, TQ), F32),
            pltpu.VMEM((HPS, TK, TQ), F32),
            pltpu.VMEM((HPS, 1, TQ), F32),
            pltpu.VMEM((HPS, DH + 16, TQ), F32),
        ],
        compiler_params=_params(),
        name="fox_attn",
    )(qT, k2, aug, vT)


def _fox_tail_kernel(x_ref, o_ref_in, mod_ref, nmlp_ref, wout_ref, w1_ref, w2_ref, out_ref):
    x = x_ref[0]
    mod = mod_ref[0]
    x1 = x + mod[2:3] * _dot(o_ref_in[0], wout_ref[...])
    out_ref[0] = _mlp_tail(x1, mod, nmlp_ref[...], w1_ref, w2_ref)


def _fox_tail(x, o, mod, nmlp, w_out, w1, w2):
    bsz, seq, _ = x.shape
    tok = pl.BlockSpec((1, TM, D), lambda b, s: (b, s, 0))
    return pl.pallas_call(
        _fox_tail_kernel,
        grid=(bsz, seq // TM),
        in_specs=[
            tok, tok,
            pl.BlockSpec((1, 6, D), lambda b, s: (b, 0, 0)),
            _const_spec((1, D)),
            _const_spec((D, D)), _const_spec((D, D_FF)), _const_spec((D_FF, D)),
        ],
        out_specs=tok,
        out_shape=jax.ShapeDtypeStruct(x.shape, F32),
        compiler_params=_params(),
        name="fox_tail",
    )(x, o, mod, nmlp.reshape(1, D), w_out.astype(BF16), w1.astype(BF16), w2.astype(BF16))


def kernel(x, c, norm_mix, norm_mlp, w_ada, b_ada, w_mlp_in, w_mlp_out, fox_w_in, fox_b_f, fox_q_norm, fox_k_norm, fox_w_out, sg_w_in, sg_ln_g, sg_ln_b, sg_w_s, sg_b_s, sg_w_out, cv_w_pw1, cv_b_pw1, cv_w_dw, cv_b_dw, cv_ln_g, cv_ln_b, cv_w_pw2, cv_b_pw2):
    depth = w_ada.shape[0]
    assert x.shape[1] % TM == 0 and x.shape[2] == D
    mods = _ada(c, w_ada, b_ada)
    for i in range(depth):
        kind, j = i % 3, i // 3
        mod = mods[i]
        if kind == 0:
            qT, vT, k2, aug = _fox_proj(x, mod, norm_mix[i], fox_w_in[j], fox_b_f[j],
                                        fox_q_norm[j], fox_k_norm[j])
            o = _fox_attn(qT, k2, aug, vT)
            x = _fox_tail(x, o, mod, norm_mlp[i], fox_w_out[j], w_mlp_in[i], w_mlp_out[i])
        elif kind == 1:
            x = _sg_layer(x, mod, norm_mix[i], norm_mlp[i], sg_w_in[j], sg_ln_g[j], sg_ln_b[j],
                          sg_w_s[j], sg_b_s[j], sg_w_out[j], w_mlp_in[i], w_mlp_out[i])
        else:
            x = _cv_layer(x, mod, norm_mix[i], norm_mlp[i], cv_w_pw1[j], cv_b_pw1[j], cv_w_dw[j],
                          cv_b_dw[j], cv_ln_g[j], cv_ln_b[j], cv_w_pw2[j], cv_b_pw2[j],
                          w_mlp_in[i], w_mlp_out[i])
    return x
```

```python
import functools

import jax
import jax.numpy as jnp
import numpy as np
from jax import lax
from jax.experimental import pallas as pl
from jax.experimental.pallas import tpu as pltpu

F32 = jnp.float32
BF16 = jnp.bfloat16

D = 1024
N_HEADS = 16
DH = D // N_HEADS
D_FF = 4 * D
EPS = 1e-6
LOG2E = 1.4426950408889634
SG_CHUNK = 128
SG_BLOCK = 64
SG_GROUPS = 8
CONV_W = 31
HALO = 32

LANES = 128
MXU = 256
TM = 512
TQ = 256
TK = 256
HPS = 4
KT_UNROLL = 4
FF_CHUNK = 1024
N_SPLIT = 3
VMEM_LIMIT = 56 * 1024 * 1024


def _const_spec(shape):
    nd = len(shape)
    return pl.BlockSpec(shape, lambda *_: (0,) * nd, pipeline_mode=pl.Buffered(1))


def _params():
    return pltpu.CompilerParams(
        dimension_semantics=("arbitrary", "arbitrary"), vmem_limit_bytes=VMEM_LIMIT)


def _dot(a, b):
    return jnp.dot(a, b, preferred_element_type=F32)


def _sigmoid(x):
    return 1.0 / (1.0 + jnp.exp(-x))


def _rms_mod(x, gain, scale, shift):
    ms = jnp.mean(x * x, axis=-1, keepdims=True)
    return (x * lax.rsqrt(ms + EPS) * gain) * (1.0 + scale) + shift


def _layer_norm(x, g, b):
    mu = jnp.mean(x, axis=-1, keepdims=True)
    xc = x - mu
    var = jnp.mean(xc * xc, axis=-1, keepdims=True)
    return xc * lax.rsqrt(var + EPS) * g + b


def _split_bf16(x):
    pieces = []
    r = x
    for _ in range(N_SPLIT):
        p = r.astype(BF16)
        pieces.append(p)
        r = r - p.astype(F32)
    return pieces


def _mlp_tail(x1, mod, nmlp, w1_ref, w2_ref):
    h = _rms_mod(x1, nmlp, mod[4:5], mod[3:4]).astype(BF16)
    acc = jnp.zeros(x1.shape, F32)
    for c in range(D_FF // FF_CHUNK):
        lo = c * FF_CHUNK
        hid = _dot(h, w1_ref[:, lo:lo + FF_CHUNK])
        hid = jnp.square(jnp.maximum(hid, 0.0)).astype(BF16)
        acc = acc + _dot(hid, w2_ref[lo:lo + FF_CHUNK, :])
    return x1 + mod[5:6] * acc


def _ada_kernel(c_ref, w_ref, b_ref, o_ref):
    c = c_ref[...]
    rows = c.shape[0]
    ca = _split_bf16(c * _sigmoid(c))
    w = w_ref[0]
    w_hi = w.astype(BF16)
    w_lo = (w - w_hi.astype(F32)).astype(BF16)
    r_hi = _dot(jnp.concatenate(ca, axis=0), w_hi)
    r_lo = _dot(jnp.concatenate(ca[:2], axis=0), w_lo)
    out = b_ref[0] + r_lo[0:rows] + r_lo[rows:2 * rows]
    for i in range(N_SPLIT):
        out = out + r_hi[i * rows:(i + 1) * rows]
    o_ref[0] = out


def _ada(c, w_ada, b_ada):
    depth, _, n = w_ada.shape
    bsz = c.shape[0]
    rows = 16
    tn = 1536
    cp = jnp.zeros((rows, D), F32).at[:bsz].set(c)
    out = pl.pallas_call(
        _ada_kernel,
        grid=(depth, n // tn),
        in_specs=[
            pl.BlockSpec((rows, D), lambda i, j: (0, 0)),
            pl.BlockSpec((1, D, tn), lambda i, j: (i, 0, j)),
            pl.BlockSpec((1, 1, tn), lambda i, j: (i, 0, j)),
        ],
        out_specs=pl.BlockSpec((1, rows, tn), lambda i, j: (i, 0, j)),
        out_shape=jax.ShapeDtypeStruct((depth, rows, n), F32),
        compiler_params=_params(),
        name="ada_mod",
    )(cp, w_ada, b_ada.reshape(depth, 1, n))
    return out[:, :bsz].reshape(depth, bsz, 6, D)


def _sg_kernel(x_ref, mod_ref, nmix_ref, nmlp_ref, win_ref, lng_ref, lnb_ref, ws_ref, bs_ref,
               wout_ref, w1_ref, w2_ref, o_ref, gated_ref):
    x = x_ref[0]
    mod = mod_ref[0]
    h = _rms_mod(x, nmix_ref[...], mod[1:2], mod[0:1]).astype(BF16)
    uv = jax.nn.gelu(_dot(h, win_ref[...]), approximate=True)
    u = uv[:, :D]
    v = _layer_norm(uv[:, D:], lng_ref[...], lnb_ref[...]).astype(BF16)

    t = lax.broadcasted_iota(jnp.int32, (SG_CHUNK, SG_CHUNK), 0) // SG_BLOCK
    s = lax.broadcasted_iota(jnp.int32, (SG_CHUNK, SG_CHUNK), 1) // SG_BLOCK
    causal = s <= t
    gd = D // SG_GROUPS
    for g in range(SG_GROUPS):
        ws = jnp.where(causal, ws_ref[g], 0.0).astype(BF16)
        bias = bs_ref[g]
        cols = slice(g * gd, (g + 1) * gd)
        for j in range(TM // (2 * SG_CHUNK)):
            r0 = j * 2 * SG_CHUNK
            r1 = r0 + SG_CHUNK
            r2 = r1 + SG_CHUNK
            rhs = jnp.concatenate([v[r0:r1, cols], v[r1:r2, cols]], axis=1)
            mix = _dot(ws, rhs)
            gated_ref[r0:r1, cols] = (u[r0:r1, cols] * (mix[:, :gd] + bias)).astype(BF16)
            gated_ref[r1:r2, cols] = (u[r1:r2, cols] * (mix[:, gd:] + bias)).astype(BF16)

    x1 = x + mod[2:3] * _dot(gated_ref[...], wout_ref[...])
    o_ref[0] = _mlp_tail(x1, mod, nmlp_ref[...], w1_ref, w2_ref)


def _sg_layer(x, mod, nmix, nmlp, w_in, ln_g, ln_b, w_s, b_s, w_out, w1, w2):
    bsz, seq, _ = x.shape
    tok = pl.BlockSpec((1, TM, D), lambda b, s: (b, s, 0))
    bs_x = jnp.broadcast_to(b_s[:, :, None], (SG_GROUPS, SG_CHUNK, D // SG_GROUPS))
    return pl.pallas_call(
        _sg_kernel,
        grid=(bsz, seq // TM),
        in_specs=[
            tok,
            pl.BlockSpec((1, 6, D), lambda b, s: (b, 0, 0)),
            _const_spec((1, D)), _const_spec((1, D)),
            _const_spec((D, 2 * D)), _const_spec((1, D)), _const_spec((1, D)),
            _const_spec((SG_GROUPS, SG_CHUNK, SG_CHUNK)),
            _const_spec((SG_GROUPS, SG_CHUNK, D // SG_GROUPS)),
            _const_spec((D, D)), _const_spec((D, D_FF)), _const_spec((D_FF, D)),
        ],
        out_specs=tok,
        out_shape=jax.ShapeDtypeStruct(x.shape, F32),
        scratch_shapes=[pltpu.VMEM((TM, D), BF16)],
        compiler_params=_params(),
        name="sg_layer",
    )(x, mod, nmix.reshape(1, D), nmlp.reshape(1, D), w_in.astype(BF16),
      ln_g.reshape(1, D), ln_b.reshape(1, D), w_s, bs_x, w_out.astype(BF16),
      w1.astype(BF16), w2.astype(BF16))


def _cv_kernel(n_s, x_ref, moda_ref, modb_ref, nmix_ref, nmlp_ref, wpw1_ref, bpw1_ref, wdw_ref,
               bdw_ref, lng_ref, lnb_ref, wpw2_ref, bpw2_ref, w1_ref, w2_ref, o_ref,
               ybuf_ref, conv_ref, xprev_ref, x1_ref, h2_ref, acc_ref):
    i = pl.program_id(0)
    n_lt = D // LANES

    @pl.when(i == 0)
    def _():
        conv_ref[...] = jnp.zeros(conv_ref.shape, F32)
        xprev_ref[...] = jnp.zeros(xprev_ref.shape, F32)

    @pl.when(i % n_s == 0)
    def _():
        ybuf_ref[:, 0:HALO, :] = jnp.zeros((n_lt, HALO, LANES), F32)

    modb = modb_ref[0]
    conv = jnp.concatenate([conv_ref[c] for c in range(n_lt)], axis=1)
    z = _layer_norm(conv, lng_ref[...], lnb_ref[...])
    z = (z * _sigmoid(z)).astype(BF16)
    x1 = xprev_ref[...] + modb[2:3] * (_dot(z, wpw2_ref[...]) + bpw2_ref[...])
    x1_ref[...] = x1
    h2_ref[...] = _rms_mod(x1, nmlp_ref[...], modb[4:5], modb[3:4]).astype(BF16)
    acc_ref[...] = jnp.zeros(acc_ref.shape, F32)

    x = x_ref[0]
    moda = moda_ref[0]
    h = _rms_mod(x, nmix_ref[...], moda[1:2], moda[0:1]).astype(BF16)
    yz = _dot(h, wpw1_ref[...]) + bpw1_ref[...]
    y = yz[:, :D] * _sigmoid(yz[:, D:])
    for c in range(n_lt):
        ybuf_ref[c, HALO:HALO + TM, :] = y[:, c * LANES:(c + 1) * LANES]
    xprev_ref[...] = x

    first = HALO - (CONV_W - 1)
    rb = 128
    win_rows = rb + HALO

    def conv_block(r, c):
        win = ybuf_ref[c, r * rb:r * rb + win_rows, :]
        acc = jnp.zeros((rb, LANES), F32) + bdw_ref[c]
        for b in range(8):
            phase = win if b == 0 else pltpu.roll(win, win_rows - b, axis=0)
            for j in range(CONV_W):
                off = first + j
                if off % 8 == b:
                    acc = acc + wdw_ref[c, j:j + 1, :] * phase[off - b:off - b + rb, :]
        conv_ref[c, r * rb:(r + 1) * rb, :] = acc

    def chunk(c, _):
        hid = jnp.maximum(_dot(h2_ref[...], w1_ref[c]).astype(BF16), 0.0)
        acc_ref[...] += _dot(hid * hid, w2_ref[c])
        for r in range(TM // rb):
            conv_block(r, c)
        return 0

    lax.fori_loop(0, n_lt, chunk, 0)
    o_ref[0] = x1_ref[...] + modb[5:6] * acc_ref[...]
    ybuf_ref[:, 0:HALO, :] = ybuf_ref[:, TM:TM + HALO, :]


def _cv_layer(x, mod, nmix, nmlp, w_pw1, b_pw1, w_dw, b_dw, ln_g, ln_b, w_pw2, b_pw2, w1, w2):
    bsz, seq, _ = x.shape
    n_s = seq // TM
    n = bsz * n_s
    n_lt = D // LANES
    ffc = D_FF // n_lt

    def cur(i):
        return jnp.minimum(i, n - 1)

    def prev(i):
        return jnp.maximum(i - 1, 0)

    wdw = jnp.zeros((HALO, D), F32).at[:CONV_W].set(w_dw).reshape(HALO, n_lt, LANES).transpose(1, 0, 2)
    bdw = b_dw.reshape(n_lt, 1, LANES)
    w1c = w1.astype(BF16).reshape(D, n_lt, ffc).transpose(1, 0, 2)
    w2c = w2.astype(BF16).reshape(n_lt, ffc, D)
    return pl.pallas_call(
        functools.partial(_cv_kernel, n_s),
        grid=(n + 1,),
        in_specs=[
            pl.BlockSpec((1, TM, D), lambda i: (cur(i) // n_s, cur(i) % n_s, 0)),
            pl.BlockSpec((1, 6, D), lambda i: (cur(i) // n_s, 0, 0)),
            pl.BlockSpec((1, 6, D), lambda i: (prev(i) // n_s, 0, 0)),
            _const_spec((1, D)), _const_spec((1, D)),
            _const_spec((D, 2 * D)), _const_spec((1, 2 * D)),
            _const_spec((n_lt, HALO, LANES)), _const_spec((n_lt, 1, LANES)),
            _const_spec((1, D)), _const_spec((1, D)),
            _const_spec((D, D)), _const_spec((1, D)),
            _const_spec((n_lt, D, ffc)), _const_spec((n_lt, ffc, D)),
        ],
        out_specs=pl.BlockSpec((1, TM, D), lambda i: (prev(i) // n_s, prev(i) % n_s, 0)),
        out_shape=jax.ShapeDtypeStruct(x.shape, F32),
        scratch_shapes=[
            pltpu.VMEM((n_lt, TM + HALO, LANES), F32),
            pltpu.VMEM((n_lt, TM, LANES), F32),
            pltpu.VMEM((TM, D), F32),
            pltpu.VMEM((TM, D), F32),
            pltpu.VMEM((TM, D), BF16),
            pltpu.VMEM((TM, D), F32),
        ],
        compiler_params=pltpu.CompilerParams(
            dimension_semantics=("arbitrary",), vmem_limit_bytes=VMEM_LIMIT),
        name="cv_layer",
    )(x, mod, mod, nmix.reshape(1, D), nmlp.reshape(1, D), w_pw1.astype(BF16),
      b_pw1.reshape(1, 2 * D), wdw, bdw, ln_g.reshape(1, D), ln_b.reshape(1, D),
      w_pw2.astype(BF16), b_pw2.reshape(1, D), w1c, w2c)


def _fox_proj_kernel(x_ref, mod_ref, nmix_ref, wqv_ref, wk_ref, wf_ref, bf_ref, gqk_ref, gsum_ref,
                     tri_ref, qT_ref, vT_ref, k2_ref, aug_ref, carry_ref):
    @pl.when(pl.program_id(1) == 0)
    def _():
        carry_ref[...] = jnp.zeros(carry_ref.shape, F32)

    x = x_ref[0]
    mod = mod_ref[0]
    h = _rms_mod(x, nmix_ref[...], mod[1:2], mod[0:1]).astype(BF16)

    f_pre = _dot(h, wf_ref[...]) + bf_ref[...]
    log_f = jnp.minimum(f_pre, 0.0) - jnp.log1p(jnp.exp(-jnp.abs(f_pre)))
    cum = carry_ref[0:1, :]
    for p in _split_bf16(log_f):
        cum = cum + _dot(tri_ref[...], p)
    carry_ref[0:1, :] = cum[TM - 1:TM, :]

    lane = lax.broadcasted_iota(jnp.int32, (TM, LANES), 1)
    aug = jnp.zeros((TM, LANES), BF16)
    for i, p in enumerate(_split_bf16(cum * -LOG2E)):
        aug = jnp.where((lane >= N_HEADS * i) & (lane < N_HEADS * (i + 1)), p, aug)
    aug_ref[0] = aug

    qvT = lax.dot_general(wqv_ref[...], h, (((1,), (1,)), ((), ())), preferred_element_type=F32)
    qT = qvT[:D].reshape(N_HEADS, DH, TM)
    qT = qT * lax.rsqrt(jnp.mean(qT * qT, axis=1, keepdims=True) + EPS)
    qT = qT.reshape(D, TM).astype(BF16)
    vT = qvT[D:].astype(BF16)
    for j in range(TM // TQ):
        qT_ref[0, j] = qT[:, j * TQ:(j + 1) * TQ]
    for j in range(TM // TK):
        vT_ref[0, j] = vT[:, j * TK:(j + 1) * TK]

    k = _dot(h, wk_ref[...])
    kk = (k * k).astype(BF16)
    ss = jnp.concatenate(
        [_dot(kk[:, c * MXU:(c + 1) * MXU], gsum_ref[...]) for c in range(D // MXU)], axis=1)
    k2_ref[0] = (k * lax.rsqrt(ss * (1.0 / DH) + EPS) * gqk_ref[...]).astype(BF16)


def _fox_consts():
    g = np.kron(np.eye(MXU // DH), np.ones((DH, DH)))
    tri = np.tril(np.ones((TM, TM)))
    return jnp.asarray(g, BF16), jnp.asarray(tri, BF16)


def _fox_proj(x, mod, nmix, w_in, b_f, q_g, k_g):
    bsz, seq, _ = x.shape
    gsum, tri = _fox_consts()
    wqv = jnp.concatenate([w_in[:, :D].T, w_in[:, 2 * D:3 * D].T], axis=0).astype(BF16)
    wk = w_in[:, D:2 * D].astype(BF16)
    n_f = N_SPLIT * N_HEADS
    wf = jnp.zeros((D, LANES), F32).at[:, :n_f].set(jnp.tile(w_in[:, 3 * D:], (1, N_SPLIT))).astype(BF16)
    bf = jnp.zeros((1, LANES), F32).at[0, :n_f].set(jnp.tile(b_f, N_SPLIT))
    gqk = jnp.tile(q_g * k_g * (LOG2E * DH ** -0.5), N_HEADS).reshape(1, D)
    tok = pl.BlockSpec((1, TM, D), lambda b, s: (b, s, 0))
    return pl.pallas_call(
        _fox_proj_kernel,
        grid=(bsz, seq // TM),
        in_specs=[
            tok,
            pl.BlockSpec((1, 6, D), lambda b, s: (b, 0, 0)),
            _const_spec((1, D)),
            _const_spec((2 * D, D)), _const_spec((D, D)), _const_spec((D, LANES)),
            _const_spec((1, LANES)), _const_spec((1, D)), _const_spec((MXU, MXU)),
            _const_spec((TM, TM)),
        ],
        out_specs=[
            pl.BlockSpec((1, TM // TQ, D, TQ), lambda b, s: (b, s, 0, 0)),
            pl.BlockSpec((1, TM // TK, D, TK), lambda b, s: (b, s, 0, 0)),
            tok,
            pl.BlockSpec((1, TM, LANES), lambda b, s: (b, s, 0)),
        ],
        out_shape=[
            jax.ShapeDtypeStruct((bsz, seq // TQ, D, TQ), BF16),
            jax.ShapeDtypeStruct((bsz, seq // TK, D, TK), BF16),
            jax.ShapeDtypeStruct((bsz, seq, D), BF16),
            jax.ShapeDtypeStruct((bsz, seq, LANES), BF16),
        ],
        scratch_shapes=[pltpu.VMEM((8, LANES), F32)],
        compiler_params=_params(),
        name="fox_proj",
    )(x, mod, nmix.reshape(1, D), wqv, wk, wf, bf, gqk, gsum, tri)


def _fox_attn_kernel(qT_ref, k_ref, aug_ref, vT_ref, o_ref, qpad_ref, sa_ref, sb_ref, m_ref, acc_ref):
    head0 = pl.program_id(1) * HPS
    row = lax.broadcasted_iota(jnp.int32, (LANES, TQ), 0)
    zeros = jnp.zeros((DH, TQ), BF16)
    ones_rows = jnp.ones((16, TK), BF16)
    key_le_query = (lax.broadcasted_iota(jnp.int32, (TK, TQ), 0)
                    <= lax.broadcasted_iota(jnp.int32, (TK, TQ), 1))

    def build_qpad(qi):
        for hd in range(HPS):
            pick = row == head0 + hd
            for i in range(1, N_SPLIT):
                pick = pick | (row == head0 + hd + N_HEADS * i)
            ones = jnp.where(pick, 1.0, 0.0).astype(BF16)
            q = qT_ref[0, qi, hd * DH:(hd + 1) * DH, :]
            parts = [q, zeros] if hd % 2 == 0 else [zeros, q]
            qpad_ref[(qi % 2) * HPS + hd] = jnp.concatenate(parts + [ones], axis=0)

    def init():
        m_ref[...] = jnp.full(m_ref.shape, -1e30, F32)
        acc_ref[...] = jnp.zeros(acc_ref.shape, F32)

    def finish(qi):
        oT = jnp.concatenate(
            [acc_ref[hd, :DH, :] / acc_ref[hd, DH:DH + 1, :] for hd in range(HPS)], axis=0)
        o_ref[0, pl.ds(pl.multiple_of(qi * TQ, TQ), TQ), :] = oT.T.astype(BF16)
        init()

    def scores(t, s_ref, q):
        k0 = pl.multiple_of(t * TK, TK)
        aug = aug_ref[0, pl.ds(k0, TK), :]
        for hd in range(HPS):
            pair = slice((hd // 2) * LANES, (hd // 2 + 1) * LANES)
            kk = jnp.concatenate([k_ref[0, pl.ds(k0, TK), pair], aug], axis=1)
            s_ref[hd] = _dot(kk, qpad_ref[(q % 2) * HPS + hd])

    def update(t, s_ref, masked):
        for hd in range(HPS):
            sT = s_ref[hd]
            if masked:
                sT = jnp.where(key_le_query, sT, -jnp.inf)
            m_old = m_ref[hd]
            m_new = jnp.maximum(m_old, jnp.max(sT, axis=0, keepdims=True))
            p = jnp.exp2(sT - m_new).astype(BF16)
            va = jnp.concatenate([vT_ref[0, t, hd * DH:(hd + 1) * DH, :], ones_rows], axis=0)
            acc_ref[hd] = acc_ref[hd] * jnp.exp2(m_old - m_new) + _dot(va, p)
            m_ref[hd] = m_new

    def tiles(base, count, q):
        for i in range(0, count, 2):
            scores(base + i + 1, sb_ref, q)
            update(base + i, sa_ref, False)
            scores(base + i + 2, sa_ref, q)
            update(base + i + 1, sb_ref, False)

    n_q = qT_ref.shape[1]

    def query_tile(qi, _):
        nxt = jnp.minimum(qi + 1, n_q - 1)
        n_main = qi // KT_UNROLL

        def main(j, _):
            tiles(j * KT_UNROLL, KT_UNROLL, qi)
            return 0

        lax.fori_loop(0, n_main, main, 0)
        left = qi - n_main * KT_UNROLL

        def pair(j, _):
            tiles(n_main * KT_UNROLL + 2 * j, 2, qi)
            return 0

        lax.fori_loop(0, left // 2, pair, 0)

        @pl.when(qi % 2 == 0)
        def _():
            update(qi, sa_ref, True)
            build_qpad(nxt)
            scores(0, sa_ref, nxt)
            finish(qi)

        @pl.when(qi % 2 == 1)
        def _():
            scores(qi, sb_ref, qi)
            update(qi - 1, sa_ref, False)
            build_qpad(nxt)
            scores(0, sa_ref, nxt)
            update(qi, sb_ref, True)
            finish(qi)

        return 0

    build_qpad(0)
    init()
    scores(0, sa_ref, 0)
    lax.fori_loop(0, n_q, query_tile, 0)


def _fox_attn(qT, k2, aug, vT):
    bsz, nq, _, _ = qT.shape
    seq = k2.shape[1]
    assert TQ == TK
    gw = HPS * DH
    return pl.pallas_call(
        _fox_attn_kernel,
        grid=(bsz, N_HEADS // HPS),
        in_specs=[
            pl.BlockSpec((1, nq, gw, TQ), lambda b, g: (b, 0, g, 0)),
            pl.BlockSpec((1, seq, gw), lambda b, g: (b, 0, g)),
            pl.BlockSpec((1, seq, LANES), lambda b, g: (b, 0, 0)),
            pl.BlockSpec((1, seq // TK, gw, TK), lambda b, g: (b, 0, g, 0)),
        ],
        out_specs=pl.BlockSpec((1, seq, gw), lambda b, g: (b, 0, g)),
        out_shape=jax.ShapeDtypeStruct((bsz, seq, D), BF16),
        scratch_shapes=[
            pltpu.VMEM((2 * HPS, MXU, TQ), BF16),
            pltpu.VMEM((HPS, TK, TQ), F32),
            pltpu.VMEM((HPS, TK, TQ), F32),
            pltpu.VMEM((HPS, 1, TQ), F32),
            pltpu.VMEM((HPS, DH + 16, TQ), F32),
        ],
        compiler_params=_params(),
        name="fox_attn",
    )(qT, k2, aug, vT)


def _fox_tail_kernel(x_ref, o_ref_in, mod_ref, nmlp_ref, wout_ref, w1_ref, w2_ref, out_ref):
    x = x_ref[0]
    mod = mod_ref[0]
    x1 = x + mod[2:3] * _dot(o_ref_in[0], wout_ref[...])
    out_ref[0] = _mlp_tail(x1, mod, nmlp_ref[...], w1_ref, w2_ref)


def _fox_tail(x, o, mod, nmlp, w_out, w1, w2):
    bsz, seq, _ = x.shape
    tok = pl.BlockSpec((1, TM, D), lambda b, s: (b, s, 0))
    return pl.pallas_call(
        _fox_tail_kernel,
        grid=(bsz, seq // TM),
        in_specs=[
            tok, tok,
            pl.BlockSpec((1, 6, D), lambda b, s: (b, 0, 0)),
            _const_spec((1, D)),
            _const_spec((D, D)), _const_spec((D, D_FF)), _const_spec((D_FF, D)),
        ],
        out_specs=tok,
        out_shape=jax.ShapeDtypeStruct(x.shape, F32),
        compiler_params=_params(),
        name="fox_tail",
    )(x, o, mod, nmlp.reshape(1, D), w_out.astype(BF16), w1.astype(BF16), w2.astype(BF16))


def kernel(x, c, norm_mix, norm_mlp, w_ada, b_ada, w_mlp_in, w_mlp_out, fox_w_in, fox_b_f, fox_q_norm, fox_k_norm, fox_w_out, sg_w_in, sg_ln_g, sg_ln_b, sg_w_s, sg_b_s, sg_w_out, cv_w_pw1, cv_b_pw1, cv_w_dw, cv_b_dw, cv_ln_g, cv_ln_b, cv_w_pw2, cv_b_pw2):
    depth = w_ada.shape[0]
    assert x.shape[1] % TM == 0 and x.shape[2] == D
    mods = _ada(c, w_ada, b_ada)
    for i in range(depth):
        kind, j = i % 3, i // 3
        mod = mods[i]
        if kind == 0:
            qT, vT, k2, aug = _fox_proj(x, mod, norm_mix[i], fox_w_in[j], fox_b_f[j],
                                        fox_q_norm[j], fox_k_norm[j])
            o = _fox_attn(qT, k2, aug, vT)
            x = _fox_tail(x, o, mod, norm_mlp[i], fox_w_out[j], w_mlp_in[i], w_mlp_out[i])
        elif kind == 1:
            x = _sg_layer(x, mod, norm_mix[i], norm_mlp[i], sg_w_in[j], sg_ln_g[j], sg_ln_b[j],
                          sg_w_s[j], sg_b_s[j], sg_w_out[j], w_mlp_in[i], w_mlp_out[i])
        else:
            x = _cv_layer(x, mod, norm_mix[i], norm_mlp[i], cv_w_pw1[j], cv_b_pw1[j], cv_w_dw[j],
                          cv_b_dw[j], cv_ln_g[j], cv_ln_b[j], cv_w_pw2[j], cv_b_pw2[j],
                          w_mlp_in[i], w_mlp_out[i])
    return x
```

```python
import functools

import jax
import jax.numpy as jnp
import numpy as np
from jax import lax
from jax.experimental import pallas as pl
from jax.experimental.pallas import tpu as pltpu

F32 = jnp.float32
BF16 = jnp.bfloat16

D = 1024
N_HEADS = 16
DH = D // N_HEADS
D_FF = 4 * D
EPS = 1e-6
LOG2E = 1.4426950408889634
SG_CHUNK = 128
SG_BLOCK = 64
SG_GROUPS = 8
CONV_W = 31
HALO = 32

LANES = 128
MXU = 256
TM = 512
TQ = 256
TK = 256
HPS = 4
KT_UNROLL = 4
FF_CHUNK = 1024
N_SPLIT = 3
VMEM_LIMIT = 56 * 1024 * 1024


def _const_spec(shape):
    nd = len(shape)
    return pl.BlockSpec(shape, lambda *_: (0,) * nd, pipeline_mode=pl.Buffered(1))


def _params():
    return pltpu.CompilerParams(
        dimension_semantics=("arbitrary", "arbitrary"), vmem_limit_bytes=VMEM_LIMIT)


def _dot(a, b):
    return jnp.dot(a, b, preferred_element_type=F32)


def _sigmoid(x):
    return 1.0 / (1.0 + jnp.exp(-x))


def _rms_mod(x, gain, scale, shift):
    ms = jnp.mean(x * x, axis=-1, keepdims=True)
    return (x * lax.rsqrt(ms + EPS) * gain) * (1.0 + scale) + shift


def _layer_norm(x, g, b):
    mu = jnp.mean(x, axis=-1, keepdims=True)
    xc = x - mu
    var = jnp.mean(xc * xc, axis=-1, keepdims=True)
    return xc * lax.rsqrt(var + EPS) * g + b


def _split_bf16(x):
    pieces = []
    r = x
    for _ in range(N_SPLIT):
        p = r.astype(BF16)
        pieces.append(p)
        r = r - p.astype(F32)
    return pieces


def _mlp_tail(x1, mod, nmlp, w1_ref, w2_ref):
    h = _rms_mod(x1, nmlp, mod[4:5], mod[3:4]).astype(BF16)
    acc = jnp.zeros(x1.shape, F32)
    for c in range(D_FF // FF_CHUNK):
        lo = c * FF_CHUNK
        hid = _dot(h, w1_ref[:, lo:lo + FF_CHUNK])
        hid = jnp.square(jnp.maximum(hid, 0.0)).astype(BF16)
        acc = acc + _dot(hid, w2_ref[lo:lo + FF_CHUNK, :])
    return x1 + mod[5:6] * acc


def _ada_kernel(c_ref, w_ref, b_ref, o_ref):
    c = c_ref[...]
    rows = c.shape[0]
    ca = _split_bf16(c * _sigmoid(c))
    w = w_ref[0]
    w_hi = w.astype(BF16)
    w_lo = (w - w_hi.astype(F32)).astype(BF16)
    r_hi = _dot(jnp.concatenate(ca, axis=0), w_hi)
    r_lo = _dot(jnp.concatenate(ca[:2], axis=0), w_lo)
    out = b_ref[0] + r_lo[0:rows] + r_lo[rows:2 * rows]
    for i in range(N_SPLIT):
        out = out + r_hi[i * rows:(i + 1) * rows]
    o_ref[0] = out


def _ada(c, w_ada, b_ada):
    depth, _, n = w_ada.shape
    bsz = c.shape[0]
    rows = 16
    tn = 3072
    cp = jnp.zeros((rows, D), F32).at[:bsz].set(c)
    out = pl.pallas_call(
        _ada_kernel,
        grid=(depth, n // tn),
        in_specs=[
            pl.BlockSpec((rows, D), lambda i, j: (0, 0)),
            pl.BlockSpec((1, D, tn), lambda i, j: (i, 0, j)),
            pl.BlockSpec((1, 1, tn), lambda i, j: (i, 0, j)),
        ],
        out_specs=pl.BlockSpec((1, rows, tn), lambda i, j: (i, 0, j)),
        out_shape=jax.ShapeDtypeStruct((depth, rows, n), F32),
        compiler_params=_params(),
        name="ada_mod",
    )(cp, w_ada, b_ada.reshape(depth, 1, n))
    return out[:, :bsz].reshape(depth, bsz, 6, D)


def _sg_kernel(x_ref, mod_ref, nmix_ref, nmlp_ref, win_ref, lng_ref, lnb_ref, ws_ref, bs_ref,
               wout_ref, w1_ref, w2_ref, o_ref, gated_ref):
    x = x_ref[0]
    mod = mod_ref[0]
    h = _rms_mod(x, nmix_ref[...], mod[1:2], mod[0:1]).astype(BF16)
    uv = jax.nn.gelu(_dot(h, win_ref[...]), approximate=True)
    u = uv[:, :D]
    v = _layer_norm(uv[:, D:], lng_ref[...], lnb_ref[...]).astype(BF16)

    t = lax.broadcasted_iota(jnp.int32, (SG_CHUNK, SG_CHUNK), 0) // SG_BLOCK
    s = lax.broadcasted_iota(jnp.int32, (SG_CHUNK, SG_CHUNK), 1) // SG_BLOCK
    causal = s <= t
    gd = D // SG_GROUPS
    for g in range(SG_GROUPS):
        ws = jnp.where(causal, ws_ref[g], 0.0).astype(BF16)
        bias = bs_ref[g]
        cols = slice(g * gd, (g + 1) * gd)
        for j in range(TM // (2 * SG_CHUNK)):
            r0 = j * 2 * SG_CHUNK
            r1 = r0 + SG_CHUNK
            r2 = r1 + SG_CHUNK
            rhs = jnp.concatenate([v[r0:r1, cols], v[r1:r2, cols]], axis=1)
            mix = _dot(ws, rhs)
            gated_ref[r0:r1, cols] = (u[r0:r1, cols] * (mix[:, :gd] + bias)).astype(BF16)
            gated_ref[r1:r2, cols] = (u[r1:r2, cols] * (mix[:, gd:] + bias)).astype(BF16)

    x1 = x + mod[2:3] * _dot(gated_ref[...], wout_ref[...])
    o_ref[0] = _mlp_tail(x1, mod, nmlp_ref[...], w1_ref, w2_ref)


def _sg_layer(x, mod, nmix, nmlp, w_in, ln_g, ln_b, w_s, b_s, w_out, w1, w2):
    bsz, seq, _ = x.shape
    tok = pl.BlockSpec((1, TM, D), lambda b, s: (b, s, 0))
    bs_x = jnp.broadcast_to(b_s[:, :, None], (SG_GROUPS, SG_CHUNK, D // SG_GROUPS))
    return pl.pallas_call(
        _sg_kernel,
        grid=(bsz, seq // TM),
        in_specs=[
            tok,
            pl.BlockSpec((1, 6, D), lambda b, s: (b, 0, 0)),
            _const_spec((1, D)), _const_spec((1, D)),
            _const_spec((D, 2 * D)), _const_spec((1, D)), _const_spec((1, D)),
            _const_spec((SG_GROUPS, SG_CHUNK, SG_CHUNK)),
            _const_spec((SG_GROUPS, SG_CHUNK, D // SG_GROUPS)),
            _const_spec((D, D)), _const_spec((D, D_FF)), _const_spec((D_FF, D)),
        ],
        out_specs=tok,
        out_shape=jax.ShapeDtypeStruct(x.shape, F32),
        scratch_shapes=[pltpu.VMEM((TM, D), BF16)],
        compiler_params=_params(),
        name="sg_layer",
    )(x, mod, nmix.reshape(1, D), nmlp.reshape(1, D), w_in.astype(BF16),
      ln_g.reshape(1, D), ln_b.reshape(1, D), w_s, bs_x, w_out.astype(BF16),
      w1.astype(BF16), w2.astype(BF16))


def _cv_kernel(n_s, x_ref, moda_ref, modb_ref, nmix_ref, nmlp_ref, wpw1_ref, bpw1_ref, wdw_ref,
               bdw_ref, lng_ref, lnb_ref, wpw2_ref, bpw2_ref, w1_ref, w2_ref, o_ref,
               ybuf_ref, conv_ref, xprev_ref, x1_ref, h2_ref, acc_ref):
    i = pl.program_id(0)
    n_lt = D // LANES

    @pl.when(i == 0)
    def _():
        conv_ref[...] = jnp.zeros(conv_ref.shape, F32)
        xprev_ref[...] = jnp.zeros(xprev_ref.shape, F32)

    @pl.when(i % n_s == 0)
    def _():
        ybuf_ref[:, 0:HALO, :] = jnp.zeros((n_lt, HALO, LANES), F32)

    modb = modb_ref[0]
    conv = jnp.concatenate([conv_ref[c] for c in range(n_lt)], axis=1)
    z = _layer_norm(conv, lng_ref[...], lnb_ref[...])
    z = (z * _sigmoid(z)).astype(BF16)
    x1 = xprev_ref[...] + modb[2:3] * (_dot(z, wpw2_ref[...]) + bpw2_ref[...])
    x1_ref[...] = x1
    h2_ref[...] = _rms_mod(x1, nmlp_ref[...], modb[4:5], modb[3:4]).astype(BF16)
    acc_ref[...] = jnp.zeros(acc_ref.shape, F32)

    x = x_ref[0]
    moda = moda_ref[0]
    h = _rms_mod(x, nmix_ref[...], moda[1:2], moda[0:1]).astype(BF16)
    yz = _dot(h, wpw1_ref[...]) + bpw1_ref[...]
    y = yz[:, :D] * _sigmoid(yz[:, D:])
    for c in range(n_lt):
        ybuf_ref[c, HALO:HALO + TM, :] = y[:, c * LANES:(c + 1) * LANES]
    xprev_ref[...] = x

    first = HALO - (CONV_W - 1)
    rb = 128
    win_rows = rb + HALO

    def conv_block(r, c):
        win = ybuf_ref[c, r * rb:r * rb + win_rows, :]
        acc = jnp.zeros((rb, LANES), F32) + bdw_ref[c]
        for b in range(8):
            phase = win if b == 0 else pltpu.roll(win, win_rows - b, axis=0)
            for j in range(CONV_W):
                off = first + j
                if off % 8 == b:
                    acc = acc + wdw_ref[c, j:j + 1, :] * phase[off - b:off - b + rb, :]
        conv_ref[c, r * rb:(r + 1) * rb, :] = acc

    def chunk(c, _):
        hid = jnp.maximum(_dot(h2_ref[...], w1_ref[c]).astype(BF16), 0.0)
        acc_ref[...] += _dot(hid * hid, w2_ref[c])
        for r in range(TM // rb):
            conv_block(r, c)
        return 0

    lax.fori_loop(0, n_lt, chunk, 0)
    o_ref[0] = x1_ref[...] + modb[5:6] * acc_ref[...]
    ybuf_ref[:, 0:HALO, :] = ybuf_ref[:, TM:TM + HALO, :]


def _cv_layer(x, mod, nmix, nmlp, w_pw1, b_pw1, w_dw, b_dw, ln_g, ln_b, w_pw2, b_pw2, w1, w2):
    bsz, seq, _ = x.shape
    n_s = seq // TM
    n = bsz * n_s
    n_lt = D // LANES
    ffc = D_FF // n_lt

    def cur(i):
        return jnp.minimum(i, n - 1)

    def prev(i):
        return jnp.maximum(i - 1, 0)

    wdw = jnp.zeros((HALO, D), F32).at[:CONV_W].set(w_dw).reshape(HALO, n_lt, LANES).transpose(1, 0, 2)
    bdw = b_dw.reshape(n_lt, 1, LANES)
    w1c = w1.astype(BF16).reshape(D, n_lt, ffc).transpose(1, 0, 2)
    w2c = w2.astype(BF16).reshape(n_lt, ffc, D)
    return pl.pallas_call(
        functools.partial(_cv_kernel, n_s),
        grid=(n + 1,),
        in_specs=[
            pl.BlockSpec((1, TM, D), lambda i: (cur(i) // n_s, cur(i) % n_s, 0)),
            pl.BlockSpec((1, 6, D), lambda i: (cur(i) // n_s, 0, 0)),
            pl.BlockSpec((1, 6, D), lambda i: (prev(i) // n_s, 0, 0)),
            _const_spec((1, D)), _const_spec((1, D)),
            _const_spec((D, 2 * D)), _const_spec((1, 2 * D)),
            _const_spec((n_lt, HALO, LANES)), _const_spec((n_lt, 1, LANES)),
            _const_spec((1, D)), _const_spec((1, D)),
            _const_spec((D, D)), _const_spec((1, D)),
            _const_spec((n_lt, D, ffc)), _const_spec((n_lt, ffc, D)),
        ],
        out_specs=pl.BlockSpec((1, TM, D), lambda i: (prev(i) // n_s, prev(i) % n_s, 0)),
        out_shape=jax.ShapeDtypeStruct(x.shape, F32),
        scratch_shapes=[
            pltpu.VMEM((n_lt, TM + HALO, LANES), F32),
            pltpu.VMEM((n_lt, TM, LANES), F32),
            pltpu.VMEM((TM, D), F32),
            pltpu.VMEM((TM, D), F32),
            pltpu.VMEM((TM, D), BF16),
            pltpu.VMEM((TM, D), F32),
        ],
        compiler_params=pltpu.CompilerParams(
            dimension_semantics=("arbitrary",), vmem_limit_bytes=VMEM_LIMIT),
        name="cv_layer",
    )(x, mod, mod, nmix.reshape(1, D), nmlp.reshape(1, D), w_pw1.astype(BF16),
      b_pw1.reshape(1, 2 * D), wdw, bdw, ln_g.reshape(1, D), ln_b.reshape(1, D),
      w_pw2.astype(BF16), b_pw2.reshape(1, D), w1c, w2c)


def _fox_proj_kernel(x_ref, mod_ref, nmix_ref, wqv_ref, wk_ref, wf_ref, bf_ref, gqk_ref, gsum_ref,
                     tri_ref, qT_ref, vT_ref, k2_ref, aug_ref, carry_ref):
    @pl.when(pl.program_id(1) == 0)
    def _():
        carry_ref[...] = jnp.zeros(carry_ref.shape, F32)

    x = x_ref[0]
    mod = mod_ref[0]
    h = _rms_mod(x, nmix_ref[...], mod[1:2], mod[0:1]).astype(BF16)

    f_pre = _dot(h, wf_ref[...]) + bf_ref[...]
    log_f = jnp.minimum(f_pre, 0.0) - jnp.log1p(jnp.exp(-jnp.abs(f_pre)))
    cum = carry_ref[0:1, :]
    for p in _split_bf16(log_f):
        cum = cum + _dot(tri_ref[...], p)
    carry_ref[0:1, :] = cum[TM - 1:TM, :]

    lane = lax.broadcasted_iota(jnp.int32, (TM, LANES), 1)
    aug = jnp.zeros((TM, LANES), BF16)
    for i, p in enumerate(_split_bf16(cum * -LOG2E)):
        aug = jnp.where((lane >= N_HEADS * i) & (lane < N_HEADS * (i + 1)), p, aug)
    aug_ref[0] = aug

    qvT = lax.dot_general(wqv_ref[...], h, (((0,), (1,)), ((), ())), preferred_element_type=F32)
    qT = qvT[:D].reshape(N_HEADS, DH, TM)
    qT = qT * lax.rsqrt(jnp.mean(qT * qT, axis=1, keepdims=True) + EPS)
    qT = qT.reshape(D, TM).astype(BF16)
    vT = qvT[D:].astype(BF16)
    for j in range(TM // TQ):
        qT_ref[0, j] = qT[:, j * TQ:(j + 1) * TQ]
    for j in range(TM // TK):
        vT_ref[0, j] = vT[:, j * TK:(j + 1) * TK]

    k = _dot(h, wk_ref[...])
    kk = (k * k).astype(BF16)
    ss = jnp.concatenate(
        [_dot(kk[:, c * MXU:(c + 1) * MXU], gsum_ref[...]) for c in range(D // MXU)], axis=1)
    k2_ref[0] = (k * lax.rsqrt(ss * (1.0 / DH) + EPS) * gqk_ref[...]).astype(BF16)


def _fox_consts():
    g = np.kron(np.eye(MXU // DH), np.ones((DH, DH)))
    tri = np.tril(np.ones((TM, TM)))
    return jnp.asarray(g, BF16), jnp.asarray(tri, BF16)


def _fox_proj(x, mod, nmix, w_in, b_f, q_g, k_g):
    bsz, seq, _ = x.shape
    gsum, tri = _fox_consts()
    wqv = jnp.concatenate([w_in[:, :D], w_in[:, 2 * D:3 * D]], axis=1).astype(BF16)
    wk = w_in[:, D:2 * D].astype(BF16)
    n_f = N_SPLIT * N_HEADS
    wf = jnp.zeros((D, LANES), F32).at[:, :n_f].set(jnp.tile(w_in[:, 3 * D:], (1, N_SPLIT))).astype(BF16)
    bf = jnp.zeros((1, LANES), F32).at[0, :n_f].set(jnp.tile(b_f, N_SPLIT))
    gqk = jnp.tile(q_g * k_g * (LOG2E * DH ** -0.5), N_HEADS).reshape(1, D)
    tok = pl.BlockSpec((1, TM, D), lambda b, s: (b, s, 0))
    return pl.pallas_call(
        _fox_proj_kernel,
        grid=(bsz, seq // TM),
        in_specs=[
            tok,
            pl.BlockSpec((1, 6, D), lambda b, s: (b, 0, 0)),
            _const_spec((1, D)),
            _const_spec((D, 2 * D)), _const_spec((D, D)), _const_spec((D, LANES)),
            _const_spec((1, LANES)), _const_spec((1, D)), _const_spec((MXU, MXU)),
            _const_spec((TM, TM)),
        ],
        out_specs=[
            pl.BlockSpec((1, TM // TQ, D, TQ), lambda b, s: (b, s, 0, 0)),
            pl.BlockSpec((1, TM // TK, D, TK), lambda b, s: (b, s, 0, 0)),
            tok,
            pl.BlockSpec((1, TM, LANES), lambda b, s: (b, s, 0)),
        ],
        out_shape=[
            jax.ShapeDtypeStruct((bsz, seq // TQ, D, TQ), BF16),
            jax.ShapeDtypeStruct((bsz, seq // TK, D, TK), BF16),
            jax.ShapeDtypeStruct((bsz, seq, D), BF16),
            jax.ShapeDtypeStruct((bsz, seq, LANES), BF16),
        ],
        scratch_shapes=[pltpu.VMEM((8, LANES), F32)],
        compiler_params=_params(),
        name="fox_proj",
    )(x, mod, nmix.reshape(1, D), wqv, wk, wf, bf, gqk, gsum, tri)


def _fox_attn_kernel(qT_ref, k_ref, aug_ref, vT_ref, o_ref, qpad_ref, sa_ref, sb_ref, m_ref, acc_ref):
    head0 = pl.program_id(1) * HPS
    row = lax.broadcasted_iota(jnp.int32, (LANES, TQ), 0)
    zeros = jnp.zeros((DH, TQ), BF16)
    ones_rows = jnp.ones((16, TK), BF16)
    key_le_query = (lax.broadcasted_iota(jnp.int32, (TK, TQ), 0)
                    <= lax.broadcasted_iota(jnp.int32, (TK, TQ), 1))

    def build_qpad(qi):
        for hd in range(HPS):
            pick = row == head0 + hd
            for i in range(1, N_SPLIT):
                pick = pick | (row == head0 + hd + N_HEADS * i)
            ones = jnp.where(pick, 1.0, 0.0).astype(BF16)
            q = qT_ref[0, qi, hd * DH:(hd + 1) * DH, :]
            parts = [q, zeros] if hd % 2 == 0 else [zeros, q]
            qpad_ref[(qi % 2) * HPS + hd] = jnp.concatenate(parts + [ones], axis=0)

    def init():
        m_ref[...] = jnp.full(m_ref.shape, -1e30, F32)
        acc_ref[...] = jnp.zeros(acc_ref.shape, F32)

    def finish(qi):
        oT = jnp.concatenate(
            [acc_ref[hd, :DH, :] / acc_ref[hd, DH:DH + 1, :] for hd in range(HPS)], axis=0)
        o_ref[0, pl.ds(pl.multiple_of(qi * TQ, TQ), TQ), :] = oT.T.astype(BF16)
        init()

    def scores(t, s_ref, q):
        k0 = pl.multiple_of(t * TK, TK)
        aug = aug_ref[0, pl.ds(k0, TK), :]
        for hd in range(HPS):
            pair = slice((hd // 2) * LANES, (hd // 2 + 1) * LANES)
            kk = jnp.concatenate([k_ref[0, pl.ds(k0, TK), pair], aug], axis=1)
            s_ref[hd] = _dot(kk, qpad_ref[(q % 2) * HPS + hd])

    def update(t, s_ref, masked):
        for hd in range(HPS):
            sT = s_ref[hd]
            if masked:
                sT = jnp.where(key_le_query, sT, -jnp.inf)
            m_old = m_ref[hd]
            m_new = jnp.maximum(m_old, jnp.max(sT, axis=0, keepdims=True))
            p = jnp.exp2(sT - m_new).astype(BF16)
            va = jnp.concatenate([vT_ref[0, t, hd * DH:(hd + 1) * DH, :], ones_rows], axis=0)
            acc_ref[hd] = acc_ref[hd] * jnp.exp2(m_old - m_new) + _dot(va, p)
            m_ref[hd] = m_new

    def tiles(base, count, q):
        for i in range(0, count, 2):
            scores(base + i + 1, sb_ref, q)
            update(base + i, sa_ref, False)
            scores(base + i + 2, sa_ref, q)
            update(base + i + 1, sb_ref, False)

    n_q = qT_ref.shape[1]

    def query_tile(qi, _):
        nxt = jnp.minimum(qi + 1, n_q - 1)
        n_main = qi // KT_UNROLL

        def main(j, _):
            tiles(j * KT_UNROLL, KT_UNROLL, qi)
            return 0

        lax.fori_loop(0, n_main, main, 0)
        left = qi - n_main * KT_UNROLL

        def pair(j, _):
            tiles(n_main * KT_UNROLL + 2 * j, 2, qi)
            return 0

        lax.fori_loop(0, left // 2, pair, 0)

        @pl.when(qi % 2 == 0)
        def _():
            update(qi, sa_ref, True)
            build_qpad(nxt)
            scores(0, sa_ref, nxt)
            finish(qi)

        @pl.when(qi % 2 == 1)
        def _():
            scores(qi, sb_ref, qi)
            update(qi - 1, sa_ref, False)
            build_qpad(nxt)
            scores(0, sa_ref, nxt)
            update(qi, sb_ref, True)
            finish(qi)

        return 0

    build_qpad(0)
    init()
    scores(0, sa_ref, 0)
    lax.fori_loop(0, n_q, query_tile, 0)


def _fox_attn(qT, k2, aug, vT):
    bsz, nq, _, _ = qT.shape
    seq = k2.shape[1]
    assert TQ == TK
    gw = HPS * DH
    return pl.pallas_call(
        _fox_attn_kernel,
        grid=(bsz, N_HEADS // HPS),
        in_specs=[
            pl.BlockSpec((1, nq, gw, TQ), lambda b, g: (b, 0, g, 0)),
            pl.BlockSpec((1, seq, gw), lambda b, g: (b, 0, g)),
            pl.BlockSpec((1, seq, LANES), lambda b, g: (b, 0, 0)),
            pl.BlockSpec((1, seq // TK, gw, TK), lambda b, g: (b, 0, g, 0)),
        ],
        out_specs=pl.BlockSpec((1, seq, gw), lambda b, g: (b, 0, g)),
        out_shape=jax.ShapeDtypeStruct((bsz, seq, D), BF16),
        scratch_shapes=[
            pltpu.VMEM((2 * HPS, MXU, TQ), BF16),
            pltpu.VMEM((HPS, TK, TQ), F32),
            pltpu.VMEM((HPS, TK, TQ), F32),
            pltpu.VMEM((HPS, 1, TQ), F32),
            pltpu.VMEM((HPS, DH + 16, TQ), F32),
        ],
        compiler_params=_params(),
        name="fox_attn",
    )(qT, k2, aug, vT)


def _fox_tail_kernel(x_ref, o_ref_in, mod_ref, nmlp_ref, wout_ref, w1_ref, w2_ref, out_ref):
    x = x_ref[0]
    mod = mod_ref[0]
    x1 = x + mod[2:3] * _dot(o_ref_in[0], wout_ref[...])
    out_ref[0] = _mlp_tail(x1, mod, nmlp_ref[...], w1_ref, w2_ref)


def _fox_tail(x, o, mod, nmlp, w_out, w1, w2):
    bsz, seq, _ = x.shape
    tok = pl.BlockSpec((1, TM, D), lambda b, s: (b, s, 0))
    return pl.pallas_call(
        _fox_tail_kernel,
        grid=(bsz, seq // TM),
        in_specs=[
            tok, tok,
            pl.BlockSpec((1, 6, D), lambda b, s: (b, 0, 0)),
            _const_spec((1, D)),
            _const_spec((D, D)), _const_spec((D, D_FF)), _const_spec((D_FF, D)),
        ],
        out_specs=tok,
        out_shape=jax.ShapeDtypeStruct(x.shape, F32),
        compiler_params=_params(),
        name="fox_tail",
    )(x, o, mod, nmlp.reshape(1, D), w_out.astype(BF16), w1.astype(BF16), w2.astype(BF16))


def kernel(x, c, norm_mix, norm_mlp, w_ada, b_ada, w_mlp_in, w_mlp_out, fox_w_in, fox_b_f, fox_q_norm, fox_k_norm, fox_w_out, sg_w_in, sg_ln_g, sg_ln_b, sg_w_s, sg_b_s, sg_w_out, cv_w_pw1, cv_b_pw1, cv_w_dw, cv_b_dw, cv_ln_g, cv_ln_b, cv_w_pw2, cv_b_pw2):
    depth = w_ada.shape[0]
    assert x.shape[1] % TM == 0 and x.shape[2] == D
    mods = _ada(c, w_ada, b_ada)
    for i in range(depth):
        kind, j = i % 3, i // 3
        mod = mods[i]
        if kind == 0:
            qT, vT, k2, aug = _fox_proj(x, mod, norm_mix[i], fox_w_in[j], fox_b_f[j],
                                        fox_q_norm[j], fox_k_norm[j])
            o = _fox_attn(qT, k2, aug, vT)
            x = _fox_tail(x, o, mod, norm_mlp[i], fox_w_out[j], w_mlp_in[i], w_mlp_out[i])
        elif kind == 1:
            x = _sg_layer(x, mod, norm_mix[i], norm_mlp[i], sg_w_in[j], sg_ln_g[j], sg_ln_b[j],
                          sg_w_s[j], sg_b_s[j], sg_w_out[j], w_mlp_in[i], w_mlp_out[i])
        else:
            x = _cv_layer(x, mod, norm_mix[i], norm_mlp[i], cv_w_pw1[j], cv_b_pw1[j], cv_w_dw[j],
                          cv_b_dw[j], cv_ln_g[j], cv_ln_b[j], cv_w_pw2[j], cv_b_pw2[j],
                          w_mlp_in[i], w_mlp_out[i])
    return x
```

```python
import functools

import jax
import jax.numpy as jnp
import numpy as np
from jax import lax
from jax.experimental import pallas as pl
from jax.experimental.pallas import tpu as pltpu

F32 = jnp.float32
BF16 = jnp.bfloat16

D = 1024
N_HEADS = 16
DH = D // N_HEADS
D_FF = 4 * D
EPS = 1e-6
LOG2E = 1.4426950408889634
SG_CHUNK = 128
SG_BLOCK = 64
SG_GROUPS = 8
CONV_W = 31
HALO = 32

LANES = 128
MXU = 256
TM = 512
TQ = 256
TK = 256
HPS = 4
KT_UNROLL = 4
FF_CHUNK = 1024
N_SPLIT = 3
VMEM_LIMIT = 56 * 1024 * 1024


def _const_spec(shape):
    nd = len(shape)
    return pl.BlockSpec(shape, lambda *_: (0,) * nd, pipeline_mode=pl.Buffered(1))


def _params():
    return pltpu.CompilerParams(
        dimension_semantics=("arbitrary", "arbitrary"), vmem_limit_bytes=VMEM_LIMIT)


def _dot(a, b):
    return jnp.dot(a, b, preferred_element_type=F32)


def _sigmoid(x):
    return 1.0 / (1.0 + jnp.exp(-x))


def _rms_mod(x, gain, scale, shift):
    ms = jnp.mean(x * x, axis=-1, keepdims=True)
    return (x * lax.rsqrt(ms + EPS) * gain) * (1.0 + scale) + shift


def _layer_norm(x, g, b):
    mu = jnp.mean(x, axis=-1, keepdims=True)
    xc = x - mu
    var = jnp.mean(xc * xc, axis=-1, keepdims=True)
    return xc * lax.rsqrt(var + EPS) * g + b


def _split_bf16(x):
    pieces = []
    r = x
    for _ in range(N_SPLIT):
        p = r.astype(BF16)
        pieces.append(p)
        r = r - p.astype(F32)
    return pieces


def _mlp_tail(x1, mod, nmlp, w1_ref, w2_ref):
    h = _rms_mod(x1, nmlp, mod[4:5], mod[3:4]).astype(BF16)
    acc = jnp.zeros(x1.shape, F32)
    for c in range(D_FF // FF_CHUNK):
        lo = c * FF_CHUNK
        hid = _dot(h, w1_ref[:, lo:lo + FF_CHUNK])
        hid = jnp.square(jnp.maximum(hid, 0.0)).astype(BF16)
        acc = acc + _dot(hid, w2_ref[lo:lo + FF_CHUNK, :])
    return x1 + mod[5:6] * acc


def _ada_kernel(c_ref, w_ref, b_ref, o_ref):
    c = c_ref[...]
    rows = c.shape[0]
    ca = _split_bf16(c * _sigmoid(c))
    w = w_ref[0]
    w_hi = w.astype(BF16)
    w_lo = (w - w_hi.astype(F32)).astype(BF16)
    r_hi = _dot(jnp.concatenate(ca, axis=0), w_hi)
    r_lo = _dot(jnp.concatenate(ca[:2], axis=0), w_lo)
    out = b_ref[0] + r_lo[0:rows] + r_lo[rows:2 * rows]
    for i in range(N_SPLIT):
        out = out + r_hi[i * rows:(i + 1) * rows]
    o_ref[0] = out


def _ada(c, w_ada, b_ada):
    depth, _, n = w_ada.shape
    bsz = c.shape[0]
    rows = 16
    tn = 3072
    cp = jnp.zeros((rows, D), F32).at[:bsz].set(c)
    out = pl.pallas_call(
        _ada_kernel,
        grid=(depth, n // tn),
        in_specs=[
            pl.BlockSpec((rows, D), lambda i, j: (0, 0)),
            pl.BlockSpec((1, D, tn), lambda i, j: (i, 0, j)),
            pl.BlockSpec((1, 1, tn), lambda i, j: (i, 0, j)),
        ],
        out_specs=pl.BlockSpec((1, rows, tn), lambda i, j: (i, 0, j)),
        out_shape=jax.ShapeDtypeStruct((depth, rows, n), F32),
        compiler_params=_params(),
        name="ada_mod",
    )(cp, w_ada, b_ada.reshape(depth, 1, n))
    return out[:, :bsz].reshape(depth, bsz, 6, D)


def _sg_kernel(x_ref, mod_ref, nmix_ref, nmlp_ref, win_ref, lng_ref, lnb_ref, ws_ref, bs_ref,
               wout_ref, w1_ref, w2_ref, o_ref, gated_ref):
    x = x_ref[0]
    mod = mod_ref[0]
    h = _rms_mod(x, nmix_ref[...], mod[1:2], mod[0:1]).astype(BF16)
    uv = jax.nn.gelu(_dot(h, win_ref[...]), approximate=True)
    u = uv[:, :D]
    v = _layer_norm(uv[:, D:], lng_ref[...], lnb_ref[...]).astype(BF16)

    t = lax.broadcasted_iota(jnp.int32, (SG_CHUNK, SG_CHUNK), 0) // SG_BLOCK
    s = lax.broadcasted_iota(jnp.int32, (SG_CHUNK, SG_CHUNK), 1) // SG_BLOCK
    causal = s <= t
    gd = D // SG_GROUPS
    for g in range(SG_GROUPS):
        ws = jnp.where(causal, ws_ref[g], 0.0).astype(BF16)
        bias = bs_ref[g]
        cols = slice(g * gd, (g + 1) * gd)
        for j in range(TM // (2 * SG_CHUNK)):
            r0 = j * 2 * SG_CHUNK
            r1 = r0 + SG_CHUNK
            r2 = r1 + SG_CHUNK
            rhs = jnp.concatenate([v[r0:r1, cols], v[r1:r2, cols]], axis=1)
            mix = _dot(ws, rhs)
            gated_ref[r0:r1, cols] = (u[r0:r1, cols] * (mix[:, :gd] + bias)).astype(BF16)
            gated_ref[r1:r2, cols] = (u[r1:r2, cols] * (mix[:, gd:] + bias)).astype(BF16)

    x1 = x + mod[2:3] * _dot(gated_ref[...], wout_ref[...])
    o_ref[0] = _mlp_tail(x1, mod, nmlp_ref[...], w1_ref, w2_ref)


def _sg_layer(x, mod, nmix, nmlp, w_in, ln_g, ln_b, w_s, b_s, w_out, w1, w2):
    bsz, seq, _ = x.shape
    tok = pl.BlockSpec((1, TM, D), lambda b, s: (b, s, 0))
    bs_x = jnp.broadcast_to(b_s[:, :, None], (SG_GROUPS, SG_CHUNK, D // SG_GROUPS))
    return pl.pallas_call(
        _sg_kernel,
        grid=(bsz, seq // TM),
        in_specs=[
            tok,
            pl.BlockSpec((1, 6, D), lambda b, s: (b, 0, 0)),
            _const_spec((1, D)), _const_spec((1, D)),
            _const_spec((D, 2 * D)), _const_spec((1, D)), _const_spec((1, D)),
            _const_spec((SG_GROUPS, SG_CHUNK, SG_CHUNK)),
            _const_spec((SG_GROUPS, SG_CHUNK, D // SG_GROUPS)),
            _const_spec((D, D)), _const_spec((D, D_FF)), _const_spec((D_FF, D)),
        ],
        out_specs=tok,
        out_shape=jax.ShapeDtypeStruct(x.shape, F32),
        scratch_shapes=[pltpu.VMEM((TM, D), BF16)],
        compiler_params=_params(),
        name="sg_layer",
    )(x, mod, nmix.reshape(1, D), nmlp.reshape(1, D), w_in.astype(BF16),
      ln_g.reshape(1, D), ln_b.reshape(1, D), w_s, bs_x, w_out.astype(BF16),
      w1.astype(BF16), w2.astype(BF16))


def _cv_kernel(n_s, x_ref, moda_ref, modb_ref, nmix_ref, nmlp_ref, wpw1_ref, bpw1_ref, wdw_ref,
               bdw_ref, lng_ref, lnb_ref, wpw2_ref, bpw2_ref, w1_ref, w2_ref, o_ref,
               ybuf_ref, conv_ref, xprev_ref, x1_ref, h2_ref, acc_ref):
    i = pl.program_id(0)
    n_lt = D // LANES

    @pl.when(i == 0)
    def _():
        conv_ref[...] = jnp.zeros(conv_ref.shape, F32)
        xprev_ref[...] = jnp.zeros(xprev_ref.shape, F32)

    @pl.when(i % n_s == 0)
    def _():
        ybuf_ref[:, 0:HALO, :] = jnp.zeros((n_lt, HALO, LANES), F32)

    modb = modb_ref[0]
    conv = jnp.concatenate([conv_ref[c] for c in range(n_lt)], axis=1)
    z = _layer_norm(conv, lng_ref[...], lnb_ref[...])
    z = (z * _sigmoid(z)).astype(BF16)
    x1 = xprev_ref[...] + modb[2:3] * (_dot(z, wpw2_ref[...]) + bpw2_ref[...])
    x1_ref[...] = x1
    h2_ref[...] = _rms_mod(x1, nmlp_ref[...], modb[4:5], modb[3:4]).astype(BF16)
    acc_ref[...] = jnp.zeros(acc_ref.shape, F32)

    x = x_ref[0]
    moda = moda_ref[0]
    h = _rms_mod(x, nmix_ref[...], moda[1:2], moda[0:1]).astype(BF16)
    yz = _dot(h, wpw1_ref[...]) + bpw1_ref[...]
    y = yz[:, :D] * _sigmoid(yz[:, D:])
    for c in range(n_lt):
        ybuf_ref[c, HALO:HALO + TM, :] = y[:, c * LANES:(c + 1) * LANES]
    xprev_ref[...] = x

    first = HALO - (CONV_W - 1)
    rb = 128
    win_rows = rb + HALO

    def conv_block(r, c):
        win = ybuf_ref[c, r * rb:r * rb + win_rows, :]
        acc = jnp.zeros((rb, LANES), F32) + bdw_ref[c]
        for b in range(8):
            phase = win if b == 0 else pltpu.roll(win, win_rows - b, axis=0)
            for j in range(CONV_W):
                off = first + j
                if off % 8 == b:
                    acc = acc + wdw_ref[c, j:j + 1, :] * phase[off - b:off - b + rb, :]
        conv_ref[c, r * rb:(r + 1) * rb, :] = acc

    def chunk(c, _):
        hid = jnp.maximum(_dot(h2_ref[...], w1_ref[c]).astype(BF16), 0.0)
        acc_ref[...] += _dot(hid * hid, w2_ref[c])
        for r in range(TM // rb):
            conv_block(r, c)
        return 0

    lax.fori_loop(0, n_lt, chunk, 0)
    o_ref[0] = x1_ref[...] + modb[5:6] * acc_ref[...]
    ybuf_ref[:, 0:HALO, :] = ybuf_ref[:, TM:TM + HALO, :]


def _cv_layer(x, mod, nmix, nmlp, w_pw1, b_pw1, w_dw, b_dw, ln_g, ln_b, w_pw2, b_pw2, w1, w2):
    bsz, seq, _ = x.shape
    n_s = seq // TM
    n = bsz * n_s
    n_lt = D // LANES
    ffc = D_FF // n_lt

    def cur(i):
        return jnp.minimum(i, n - 1)

    def prev(i):
        return jnp.maximum(i - 1, 0)

    wdw = jnp.zeros((HALO, D), F32).at[:CONV_W].set(w_dw).reshape(HALO, n_lt, LANES).transpose(1, 0, 2)
    bdw = b_dw.reshape(n_lt, 1, LANES)
    w1c = w1.astype(BF16).reshape(D, n_lt, ffc).transpose(1, 0, 2)
    w2c = w2.astype(BF16).reshape(n_lt, ffc, D)
    return pl.pallas_call(
        functools.partial(_cv_kernel, n_s),
        grid=(n + 1,),
        in_specs=[
            pl.BlockSpec((1, TM, D), lambda i: (cur(i) // n_s, cur(i) % n_s, 0)),
            pl.BlockSpec((1, 6, D), lambda i: (cur(i) // n_s, 0, 0)),
            pl.BlockSpec((1, 6, D), lambda i: (prev(i) // n_s, 0, 0)),
            _const_spec((1, D)), _const_spec((1, D)),
            _const_spec((D, 2 * D)), _const_spec((1, 2 * D)),
            _const_spec((n_lt, HALO, LANES)), _const_spec((n_lt, 1, LANES)),
            _const_spec((1, D)), _const_spec((1, D)),
            _const_spec((D, D)), _const_spec((1, D)),
            _const_spec((n_lt, D, ffc)), _const_spec((n_lt, ffc, D)),
        ],
        out_specs=pl.BlockSpec((1, TM, D), lambda i: (prev(i) // n_s, prev(i) % n_s, 0)),
        out_shape=jax.ShapeDtypeStruct(x.shape, F32),
        scratch_shapes=[
            pltpu.VMEM((n_lt, TM + HALO, LANES), F32),
            pltpu.VMEM((n_lt, TM, LANES), F32),
            pltpu.VMEM((TM, D), F32),
            pltpu.VMEM((TM, D), F32),
            pltpu.VMEM((TM, D), BF16),
            pltpu.VMEM((TM, D), F32),
        ],
        compiler_params=pltpu.CompilerParams(
            dimension_semantics=("arbitrary",), vmem_limit_bytes=VMEM_LIMIT),
        name="cv_layer",
    )(x, mod, mod, nmix.reshape(1, D), nmlp.reshape(1, D), w_pw1.astype(BF16),
      b_pw1.reshape(1, 2 * D), wdw, bdw, ln_g.reshape(1, D), ln_b.reshape(1, D),
      w_pw2.astype(BF16), b_pw2.reshape(1, D), w1c, w2c)


def _fox_proj_kernel(x_ref, mod_ref, nmix_ref, w_ref, wf_ref, bf_ref, gqk_ref, gsum_ref,
                     tri_ref, qT_ref, vT_ref, k2_ref, aug_ref, carry_ref):
    @pl.when(pl.program_id(1) == 0)
    def _():
        carry_ref[...] = jnp.zeros(carry_ref.shape, F32)

    x = x_ref[0]
    mod = mod_ref[0]
    h = _rms_mod(x, nmix_ref[...], mod[1:2], mod[0:1]).astype(BF16)

    f_pre = _dot(h, wf_ref[...]) + bf_ref[...]
    log_f = jnp.minimum(f_pre, 0.0) - jnp.log1p(jnp.exp(-jnp.abs(f_pre)))
    cum = carry_ref[0:1, :]
    for p in _split_bf16(log_f):
        cum = cum + _dot(tri_ref[...], p)
    carry_ref[0:1, :] = cum[TM - 1:TM, :]

    lane = lax.broadcasted_iota(jnp.int32, (TM, LANES), 1)
    aug = jnp.zeros((TM, LANES), BF16)
    for i, p in enumerate(_split_bf16(cum * -LOG2E)):
        aug = jnp.where((lane >= N_HEADS * i) & (lane < N_HEADS * (i + 1)), p, aug)
    aug_ref[0] = aug

    tn = (((0,), (1,)), ((), ()))
    qT = lax.dot_general(w_ref[:, 0:D], h, tn, preferred_element_type=F32).reshape(N_HEADS, DH, TM)
    qT = qT * lax.rsqrt(jnp.mean(qT * qT, axis=1, keepdims=True) + EPS)
    qT = qT.reshape(D, TM).astype(BF16)
    vT = lax.dot_general(w_ref[:, 2 * D:3 * D], h, tn, preferred_element_type=F32).astype(BF16)
    for j in range(TM // TQ):
        qT_ref[0, j] = qT[:, j * TQ:(j + 1) * TQ]
    for j in range(TM // TK):
        vT_ref[0, j] = vT[:, j * TK:(j + 1) * TK]

    k = _dot(h, w_ref[:, D:2 * D])
    kk = (k * k).astype(BF16)
    ss = jnp.concatenate(
        [_dot(kk[:, c * MXU:(c + 1) * MXU], gsum_ref[...]) for c in range(D // MXU)], axis=1)
    k2_ref[0] = (k * lax.rsqrt(ss * (1.0 / DH) + EPS) * gqk_ref[...]).astype(BF16)


def _fox_consts():
    g = np.kron(np.eye(MXU // DH), np.ones((DH, DH)))
    tri = np.tril(np.ones((TM, TM)))
    return jnp.asarray(g, BF16), jnp.asarray(tri, BF16)


def _fox_proj(x, mod, nmix, w_in, b_f, q_g, k_g):
    bsz, seq, _ = x.shape
    gsum, tri = _fox_consts()
    n_f = N_SPLIT * N_HEADS
    wf = jnp.zeros((D, LANES), F32).at[:, :n_f].set(jnp.tile(w_in[:, 3 * D:], (1, N_SPLIT))).astype(BF16)
    bf = jnp.zeros((1, LANES), F32).at[0, :n_f].set(jnp.tile(b_f, N_SPLIT))
    gqk = jnp.tile(q_g * k_g * (LOG2E * DH ** -0.5), N_HEADS).reshape(1, D)
    tok = pl.BlockSpec((1, TM, D), lambda b, s: (b, s, 0))
    return pl.pallas_call(
        _fox_proj_kernel,
        grid=(bsz, seq // TM),
        in_specs=[
            tok,
            pl.BlockSpec((1, 6, D), lambda b, s: (b, 0, 0)),
            _const_spec((1, D)),
            _const_spec(w_in.shape), _const_spec((D, LANES)),
            _const_spec((1, LANES)), _const_spec((1, D)), _const_spec((MXU, MXU)),
            _const_spec((TM, TM)),
        ],
        out_specs=[
            pl.BlockSpec((1, TM // TQ, D, TQ), lambda b, s: (b, s, 0, 0)),
            pl.BlockSpec((1, TM // TK, D, TK), lambda b, s: (b, s, 0, 0)),
            tok,
            pl.BlockSpec((1, TM, LANES), lambda b, s: (b, s, 0)),
        ],
        out_shape=[
            jax.ShapeDtypeStruct((bsz, seq // TQ, D, TQ), BF16),
            jax.ShapeDtypeStruct((bsz, seq // TK, D, TK), BF16),
            jax.ShapeDtypeStruct((bsz, seq, D), BF16),
            jax.ShapeDtypeStruct((bsz, seq, LANES), BF16),
        ],
        scratch_shapes=[pltpu.VMEM((8, LANES), F32)],
        compiler_params=_params(),
        name="fox_proj",
    )(x, mod, nmix.reshape(1, D), w_in.astype(BF16), wf, bf, gqk, gsum, tri)


def _fox_attn_kernel(qT_ref, k_ref, aug_ref, vT_ref, o_ref, qpad_ref, sa_ref, sb_ref, m_ref, acc_ref):
    head0 = pl.program_id(1) * HPS
    row = lax.broadcasted_iota(jnp.int32, (LANES, TQ), 0)
    zeros = jnp.zeros((DH, TQ), BF16)
    ones_rows = jnp.ones((16, TK), BF16)
    key_le_query = (lax.broadcasted_iota(jnp.int32, (TK, TQ), 0)
                    <= lax.broadcasted_iota(jnp.int32, (TK, TQ), 1))

    def build_qpad(qi):
        for hd in range(HPS):
            pick = row == head0 + hd
            for i in range(1, N_SPLIT):
                pick = pick | (row == head0 + hd + N_HEADS * i)
            ones = jnp.where(pick, 1.0, 0.0).astype(BF16)
            q = qT_ref[0, qi, hd * DH:(hd + 1) * DH, :]
            parts = [q, zeros] if hd % 2 == 0 else [zeros, q]
            qpad_ref[(qi % 2) * HPS + hd] = jnp.concatenate(parts + [ones], axis=0)

    def init():
        m_ref[...] = jnp.full(m_ref.shape, -1e30, F32)
        acc_ref[...] = jnp.zeros(acc_ref.shape, F32)

    def finish(qi):
        oT = jnp.concatenate(
            [acc_ref[hd, :DH, :] / acc_ref[hd, DH:DH + 1, :] for hd in range(HPS)], axis=0)
        o_ref[0, pl.ds(pl.multiple_of(qi * TQ, TQ), TQ), :] = oT.T.astype(BF16)
        init()

    def scores(t, s_ref, q):
        k0 = pl.multiple_of(t * TK, TK)
        aug = aug_ref[0, pl.ds(k0, TK), :]
        for hd in range(HPS):
            pair = slice((hd // 2) * LANES, (hd // 2 + 1) * LANES)
            kk = jnp.concatenate([k_ref[0, pl.ds(k0, TK), pair], aug], axis=1)
            s_ref[hd] = _dot(kk, qpad_ref[(q % 2) * HPS + hd])

    def update(t, s_ref, masked):
        for hd in range(HPS):
            sT = s_ref[hd]
            if masked:
                sT = jnp.where(key_le_query, sT, -jnp.inf)
            m_old = m_ref[hd]
            m_new = jnp.maximum(m_old, jnp.max(sT, axis=0, keepdims=True))
            p = jnp.exp2(sT - m_new).astype(BF16)
            va = jnp.concatenate([vT_ref[0, t, hd * DH:(hd + 1) * DH, :], ones_rows], axis=0)
            acc_ref[hd] = acc_ref[hd] * jnp.exp2(m_old - m_new) + _dot(va, p)
            m_ref[hd] = m_new

    def tiles(base, count, q):
        for i in range(0, count, 2):
            scores(base + i + 1, sb_ref, q)
            update(base + i, sa_ref, False)
            scores(base + i + 2, sa_ref, q)
            update(base + i + 1, sb_ref, False)

    n_q = qT_ref.shape[1]

    def query_tile(qi, _):
        nxt = jnp.minimum(qi + 1, n_q - 1)
        n_main = qi // KT_UNROLL

        def main(j, _):
            tiles(j * KT_UNROLL, KT_UNROLL, qi)
            return 0

        lax.fori_loop(0, n_main, main, 0)
        left = qi - n_main * KT_UNROLL

        def pair(j, _):
            tiles(n_main * KT_UNROLL + 2 * j, 2, qi)
            return 0

        lax.fori_loop(0, left // 2, pair, 0)

        @pl.when(qi % 2 == 0)
        def _():
            update(qi, sa_ref, True)
            build_qpad(nxt)
            scores(0, sa_ref, nxt)
            finish(qi)

        @pl.when(qi % 2 == 1)
        def _():
            scores(qi, sb_ref, qi)
            update(qi - 1, sa_ref, False)
            build_qpad(nxt)
            scores(0, sa_ref, nxt)
            update(qi, sb_ref, True)
            finish(qi)

        return 0

    build_qpad(0)
    init()
    scores(0, sa_ref, 0)
    lax.fori_loop(0, n_q, query_tile, 0)


def _fox_attn(qT, k2, aug, vT):
    bsz, nq, _, _ = qT.shape
    seq = k2.shape[1]
    assert TQ == TK
    gw = HPS * DH
    return pl.pallas_call(
        _fox_attn_kernel,
        grid=(bsz, N_HEADS // HPS),
        in_specs=[
            pl.BlockSpec((1, nq, gw, TQ), lambda b, g: (b, 0, g, 0)),
            pl.BlockSpec((1, seq, gw), lambda b, g: (b, 0, g)),
            pl.BlockSpec((1, seq, LANES), lambda b, g: (b, 0, 0)),
            pl.BlockSpec((1, seq // TK, gw, TK), lambda b, g: (b, 0, g, 0)),
        ],
        out_specs=pl.BlockSpec((1, seq, gw), lambda b, g: (b, 0, g)),
        out_shape=jax.ShapeDtypeStruct((bsz, seq, D), BF16),
        scratch_shapes=[
            pltpu.VMEM((2 * HPS, MXU, TQ), BF16),
            pltpu.VMEM((HPS, TK, TQ), F32),
            pltpu.VMEM((HPS, TK, TQ), F32),
            pltpu.VMEM((HPS, 1, TQ), F32),
            pltpu.VMEM((HPS, DH + 16, TQ), F32),
        ],
        compiler_params=_params(),
        name="fox_attn",
    )(qT, k2, aug, vT)


def _fox_tail_kernel(x_ref, o_ref_in, mod_ref, nmlp_ref, wout_ref, w1_ref, w2_ref, out_ref):
    x = x_ref[0]
    mod = mod_ref[0]
    x1 = x + mod[2:3] * _dot(o_ref_in[0], wout_ref[...])
    out_ref[0] = _mlp_tail(x1, mod, nmlp_ref[...], w1_ref, w2_ref)


def _fox_tail(x, o, mod, nmlp, w_out, w1, w2):
    bsz, seq, _ = x.shape
    tok = pl.BlockSpec((1, TM, D), lambda b, s: (b, s, 0))
    return pl.pallas_call(
        _fox_tail_kernel,
        grid=(bsz, seq // TM),
        in_specs=[
            tok, tok,
            pl.BlockSpec((1, 6, D), lambda b, s: (b, 0, 0)),
            _const_spec((1, D)),
            _const_spec((D, D)), _const_spec((D, D_FF)), _const_spec((D_FF, D)),
        ],
        out_specs=tok,
        out_shape=jax.ShapeDtypeStruct(x.shape, F32),
        compiler_params=_params(),
        name="fox_tail",
    )(x, o, mod, nmlp.reshape(1, D), w_out.astype(BF16), w1.astype(BF16), w2.astype(BF16))


def kernel(x, c, norm_mix, norm_mlp, w_ada, b_ada, w_mlp_in, w_mlp_out, fox_w_in, fox_b_f, fox_q_norm, fox_k_norm, fox_w_out, sg_w_in, sg_ln_g, sg_ln_b, sg_w_s, sg_b_s, sg_w_out, cv_w_pw1, cv_b_pw1, cv_w_dw, cv_b_dw, cv_ln_g, cv_ln_b, cv_w_pw2, cv_b_pw2):
    depth = w_ada.shape[0]
    assert x.shape[1] % TM == 0 and x.shape[2] == D
    mods = _ada(c, w_ada, b_ada)
    for i in range(depth):
        kind, j = i % 3, i // 3
        mod = mods[i]
        if kind == 0:
            qT, vT, k2, aug = _fox_proj(x, mod, norm_mix[i], fox_w_in[j], fox_b_f[j],
                                        fox_q_norm[j], fox_k_norm[j])
            o = _fox_attn(qT, k2, aug, vT)
            x = _fox_tail(x, o, mod, norm_mlp[i], fox_w_out[j], w_mlp_in[i], w_mlp_out[i])
        elif kind == 1:
            x = _sg_layer(x, mod, norm_mix[i], norm_mlp[i], sg_w_in[j], sg_ln_g[j], sg_ln_b[j],
                          sg_w_s[j], sg_b_s[j], sg_w_out[j], w_mlp_in[i], w_mlp_out[i])
        else:
            x = _cv_layer(x, mod, norm_mix[i], norm_mlp[i], cv_w_pw1[j], cv_b_pw1[j], cv_w_dw[j],
                          cv_b_dw[j], cv_ln_g[j], cv_ln_b[j], cv_w_pw2[j], cv_b_pw2[j],
                          w_mlp_in[i], w_mlp_out[i])
    return x
```

```python
import functools

import jax
import jax.numpy as jnp
import numpy as np
from jax import lax
from jax.experimental import pallas as pl
from jax.experimental.pallas import tpu as pltpu

F32 = jnp.float32
BF16 = jnp.bfloat16

D = 1024
N_HEADS = 16
DH = D // N_HEADS
D_FF = 4 * D
EPS = 1e-6
LOG2E = 1.4426950408889634
SG_CHUNK = 128
SG_BLOCK = 64
SG_GROUPS = 8
CONV_W = 31
HALO = 32

LANES = 128
MXU = 256
TM = 512
TQ = 256
TK = 256
HPS = 4
KT_UNROLL = 4
FF_CHUNK = 1024
N_SPLIT = 3
VMEM_LIMIT = 56 * 1024 * 1024


def _const_spec(shape):
    nd = len(shape)
    return pl.BlockSpec(shape, lambda *_: (0,) * nd, pipeline_mode=pl.Buffered(1))


def _layer_spec(shape, layer):
    nd = len(shape)
    return pl.BlockSpec((1,) + tuple(shape[1:]), lambda *_: (layer,) + (0,) * (nd - 1),
                        pipeline_mode=pl.Buffered(1))


def _params():
    return pltpu.CompilerParams(
        dimension_semantics=("arbitrary", "arbitrary"), vmem_limit_bytes=VMEM_LIMIT)


def _dot(a, b):
    return jnp.dot(a, b, preferred_element_type=F32)


def _sigmoid(x):
    return 1.0 / (1.0 + jnp.exp(-x))


def _rms_mod(x, gain, scale, shift):
    ms = jnp.mean(x * x, axis=-1, keepdims=True)
    return (x * lax.rsqrt(ms + EPS) * gain) * (1.0 + scale) + shift


def _layer_norm(x, g, b):
    mu = jnp.mean(x, axis=-1, keepdims=True)
    xc = x - mu
    var = jnp.mean(xc * xc, axis=-1, keepdims=True)
    return xc * lax.rsqrt(var + EPS) * g + b


def _split_bf16(x):
    pieces = []
    r = x
    for _ in range(N_SPLIT):
        p = r.astype(BF16)
        pieces.append(p)
        r = r - p.astype(F32)
    return pieces


def _mlp_tail(x1, mod, nmlp, w1_ref, w2_ref):
    w1_ref, w2_ref = w1_ref.at[0], w2_ref.at[0]
    h = _rms_mod(x1, nmlp, mod[4:5], mod[3:4]).astype(BF16)
    acc = jnp.zeros(x1.shape, F32)
    for c in range(D_FF // FF_CHUNK):
        lo = c * FF_CHUNK
        hid = _dot(h, w1_ref[:, lo:lo + FF_CHUNK])
        hid = jnp.square(jnp.maximum(hid, 0.0)).astype(BF16)
        acc = acc + _dot(hid, w2_ref[lo:lo + FF_CHUNK, :])
    return x1 + mod[5:6] * acc


def _ada_kernel(c_ref, w_ref, b_ref, o_ref):
    c = c_ref[...]
    rows = c.shape[0]
    ca = _split_bf16(c * _sigmoid(c))
    w = w_ref[0]
    w_hi = w.astype(BF16)
    w_lo = (w - w_hi.astype(F32)).astype(BF16)
    r_hi = _dot(jnp.concatenate(ca, axis=0), w_hi)
    r_lo = _dot(jnp.concatenate(ca[:2], axis=0), w_lo)
    out = b_ref[0] + r_lo[0:rows] + r_lo[rows:2 * rows]
    for i in range(N_SPLIT):
        out = out + r_hi[i * rows:(i + 1) * rows]
    o_ref[0] = out


def _ada(c, w_ada, b_ada):
    depth, _, n = w_ada.shape
    bsz = c.shape[0]
    rows = 16
    tn = 3072
    cp = jnp.zeros((rows, D), F32).at[:bsz].set(c)
    out = pl.pallas_call(
        _ada_kernel,
        grid=(depth, n // tn),
        in_specs=[
            pl.BlockSpec((rows, D), lambda i, j: (0, 0)),
            pl.BlockSpec((1, D, tn), lambda i, j: (i, 0, j)),
            pl.BlockSpec((1, 1, tn), lambda i, j: (i, 0, j)),
        ],
        out_specs=pl.BlockSpec((1, rows, tn), lambda i, j: (i, 0, j)),
        out_shape=jax.ShapeDtypeStruct((depth, rows, n), F32),
        compiler_params=_params(),
        name="ada_mod",
    )(cp, w_ada, b_ada.reshape(depth, 1, n))
    return out[:, :bsz].reshape(depth, bsz, 6, D)


def _sg_kernel(x_ref, mod_ref, nmix_ref, nmlp_ref, win_ref, lng_ref, lnb_ref, ws_ref, bs_ref,
               wout_ref, w1_ref, w2_ref, o_ref, gated_ref):
    x = x_ref[0]
    mod = mod_ref[0]
    h = _rms_mod(x, nmix_ref[...], mod[1:2], mod[0:1]).astype(BF16)
    uv = jax.nn.gelu(_dot(h, win_ref[...]), approximate=True)
    u = uv[:, :D]
    v = _layer_norm(uv[:, D:], lng_ref[...], lnb_ref[...]).astype(BF16)

    t = lax.broadcasted_iota(jnp.int32, (SG_CHUNK, SG_CHUNK), 0) // SG_BLOCK
    s = lax.broadcasted_iota(jnp.int32, (SG_CHUNK, SG_CHUNK), 1) // SG_BLOCK
    causal = s <= t
    gd = D // SG_GROUPS
    for g in range(SG_GROUPS):
        ws = jnp.where(causal, ws_ref[g], 0.0).astype(BF16)
        bias = bs_ref[g]
        cols = slice(g * gd, (g + 1) * gd)
        for j in range(TM // (2 * SG_CHUNK)):
            r0 = j * 2 * SG_CHUNK
            r1 = r0 + SG_CHUNK
            r2 = r1 + SG_CHUNK
            rhs = jnp.concatenate([v[r0:r1, cols], v[r1:r2, cols]], axis=1)
            mix = _dot(ws, rhs)
            gated_ref[r0:r1, cols] = (u[r0:r1, cols] * (mix[:, :gd] + bias)).astype(BF16)
            gated_ref[r1:r2, cols] = (u[r1:r2, cols] * (mix[:, gd:] + bias)).astype(BF16)

    x1 = x + mod[2:3] * _dot(gated_ref[...], wout_ref[...])
    o_ref[0] = _mlp_tail(x1, mod, nmlp_ref[...], w1_ref, w2_ref)


def _sg_layer(x, mod, nmix, nmlp, w_in, ln_g, ln_b, w_s, b_s, w_out, w1_all, w2_all, layer):
    bsz, seq, _ = x.shape
    tok = pl.BlockSpec((1, TM, D), lambda b, s: (b, s, 0))
    bs_x = jnp.broadcast_to(b_s[:, :, None], (SG_GROUPS, SG_CHUNK, D // SG_GROUPS))
    return pl.pallas_call(
        _sg_kernel,
        grid=(bsz, seq // TM),
        in_specs=[
            tok,
            pl.BlockSpec((1, 6, D), lambda b, s: (b, 0, 0)),
            _const_spec((1, D)), _const_spec((1, D)),
            _const_spec((D, 2 * D)), _const_spec((1, D)), _const_spec((1, D)),
            _const_spec((SG_GROUPS, SG_CHUNK, SG_CHUNK)),
            _const_spec((SG_GROUPS, SG_CHUNK, D // SG_GROUPS)),
            _const_spec((D, D)), _layer_spec(w1_all.shape, layer), _layer_spec(w2_all.shape, layer),
        ],
        out_specs=tok,
        out_shape=jax.ShapeDtypeStruct(x.shape, F32),
        scratch_shapes=[pltpu.VMEM((TM, D), BF16)],
        compiler_params=_params(),
        name="sg_layer",
    )(x, mod, nmix.reshape(1, D), nmlp.reshape(1, D), w_in.astype(BF16),
      ln_g.reshape(1, D), ln_b.reshape(1, D), w_s, bs_x, w_out.astype(BF16), w1_all, w2_all)


def _cv_kernel(n_s, x_ref, moda_ref, modb_ref, nmix_ref, nmlp_ref, wpw1_ref, bpw1_ref, wdw_ref,
               bdw_ref, lng_ref, lnb_ref, wpw2_ref, bpw2_ref, w1_ref, w2_ref, o_ref,
               ybuf_ref, conv_ref, xprev_ref, x1_ref, h2_ref, acc_ref):
    i = pl.program_id(0)
    n_lt = D // LANES

    @pl.when(i == 0)
    def _():
        conv_ref[...] = jnp.zeros(conv_ref.shape, F32)
        xprev_ref[...] = jnp.zeros(xprev_ref.shape, F32)

    @pl.when(i % n_s == 0)
    def _():
        ybuf_ref[:, 0:HALO, :] = jnp.zeros((n_lt, HALO, LANES), F32)

    modb = modb_ref[0]
    conv = jnp.concatenate([conv_ref[c] for c in range(n_lt)], axis=1)
    z = _layer_norm(conv, lng_ref[...], lnb_ref[...])
    z = (z * _sigmoid(z)).astype(BF16)
    x1 = xprev_ref[...] + modb[2:3] * (_dot(z, wpw2_ref[...]) + bpw2_ref[...])
    x1_ref[...] = x1
    h2_ref[...] = _rms_mod(x1, nmlp_ref[...], modb[4:5], modb[3:4]).astype(BF16)
    acc_ref[...] = jnp.zeros(acc_ref.shape, F32)

    x = x_ref[0]
    moda = moda_ref[0]
    h = _rms_mod(x, nmix_ref[...], moda[1:2], moda[0:1]).astype(BF16)
    yz = _dot(h, wpw1_ref[...]) + bpw1_ref[...]
    y = yz[:, :D] * _sigmoid(yz[:, D:])
    for c in range(n_lt):
        ybuf_ref[c, HALO:HALO + TM, :] = y[:, c * LANES:(c + 1) * LANES]
    xprev_ref[...] = x

    first = HALO - (CONV_W - 1)
    rb = 128
    win_rows = rb + HALO

    def conv_block(r, c):
        win = ybuf_ref[c, r * rb:r * rb + win_rows, :]
        acc = jnp.zeros((rb, LANES), F32) + bdw_ref[c]
        for b in range(8):
            phase = win if b == 0 else pltpu.roll(win, win_rows - b, axis=0)
            for j in range(CONV_W):
                off = first + j
                if off % 8 == b:
                    acc = acc + wdw_ref[c, j:j + 1, :] * phase[off - b:off - b + rb, :]
        conv_ref[c, r * rb:(r + 1) * rb, :] = acc

    def chunk(c, _):
        hid = jnp.maximum(_dot(h2_ref[...], w1_ref[c]).astype(BF16), 0.0)
        acc_ref[...] += _dot(hid * hid, w2_ref[c])
        for r in range(TM // rb):
            conv_block(r, c)
        return 0

    lax.fori_loop(0, n_lt, chunk, 0)
    o_ref[0] = x1_ref[...] + modb[5:6] * acc_ref[...]
    ybuf_ref[:, 0:HALO, :] = ybuf_ref[:, TM:TM + HALO, :]


def _cv_layer(x, mod, nmix, nmlp, w_pw1, b_pw1, w_dw, b_dw, ln_g, ln_b, w_pw2, b_pw2, w1, w2):
    bsz, seq, _ = x.shape
    n_s = seq // TM
    n = bsz * n_s
    n_lt = D // LANES
    ffc = D_FF // n_lt

    def cur(i):
        return jnp.minimum(i, n - 1)

    def prev(i):
        return jnp.maximum(i - 1, 0)

    wdw = jnp.zeros((HALO, D), F32).at[:CONV_W].set(w_dw).reshape(HALO, n_lt, LANES).transpose(1, 0, 2)
    bdw = b_dw.reshape(n_lt, 1, LANES)
    w1c = w1.astype(BF16).reshape(D, n_lt, ffc).transpose(1, 0, 2)
    w2c = w2.astype(BF16).reshape(n_lt, ffc, D)
    return pl.pallas_call(
        functools.partial(_cv_kernel, n_s),
        grid=(n + 1,),
        in_specs=[
            pl.BlockSpec((1, TM, D), lambda i: (cur(i) // n_s, cur(i) % n_s, 0)),
            pl.BlockSpec((1, 6, D), lambda i: (cur(i) // n_s, 0, 0)),
            pl.BlockSpec((1, 6, D), lambda i: (prev(i) // n_s, 0, 0)),
            _const_spec((1, D)), _const_spec((1, D)),
            _const_spec((D, 2 * D)), _const_spec((1, 2 * D)),
            _const_spec((n_lt, HALO, LANES)), _const_spec((n_lt, 1, LANES)),
            _const_spec((1, D)), _const_spec((1, D)),
            _const_spec((D, D)), _const_spec((1, D)),
            _const_spec((n_lt, D, ffc)), _const_spec((n_lt, ffc, D)),
        ],
        out_specs=pl.BlockSpec((1, TM, D), lambda i: (prev(i) // n_s, prev(i) % n_s, 0)),
        out_shape=jax.ShapeDtypeStruct(x.shape, F32),
        scratch_shapes=[
            pltpu.VMEM((n_lt, TM + HALO, LANES), F32),
            pltpu.VMEM((n_lt, TM, LANES), F32),
            pltpu.VMEM((TM, D), F32),
            pltpu.VMEM((TM, D), F32),
            pltpu.VMEM((TM, D), BF16),
            pltpu.VMEM((TM, D), F32),
        ],
        compiler_params=pltpu.CompilerParams(
            dimension_semantics=("arbitrary",), vmem_limit_bytes=VMEM_LIMIT),
        name="cv_layer",
    )(x, mod, mod, nmix.reshape(1, D), nmlp.reshape(1, D), w_pw1.astype(BF16),
      b_pw1.reshape(1, 2 * D), wdw, bdw, ln_g.reshape(1, D), ln_b.reshape(1, D),
      w_pw2.astype(BF16), b_pw2.reshape(1, D), w1c, w2c)


def _fox_proj_kernel(x_ref, mod_ref, nmix_ref, w_ref, wf_ref, bf_ref, gqk_ref, gsum_ref,
                     tri_ref, qT_ref, vT_ref, k2_ref, aug_ref, carry_ref):
    @pl.when(pl.program_id(1) == 0)
    def _():
        carry_ref[...] = jnp.zeros(carry_ref.shape, F32)

    x = x_ref[0]
    mod = mod_ref[0]
    h = _rms_mod(x, nmix_ref[...], mod[1:2], mod[0:1]).astype(BF16)

    f_pre = _dot(h, wf_ref[...]) + bf_ref[...]
    log_f = jnp.minimum(f_pre, 0.0) - jnp.log1p(jnp.exp(-jnp.abs(f_pre)))
    cum = carry_ref[0:1, :]
    for p in _split_bf16(log_f):
        cum = cum + _dot(tri_ref[...], p)
    carry_ref[0:1, :] = cum[TM - 1:TM, :]

    lane = lax.broadcasted_iota(jnp.int32, (TM, LANES), 1)
    aug = jnp.zeros((TM, LANES), BF16)
    for i, p in enumerate(_split_bf16(cum * -LOG2E)):
        aug = jnp.where((lane >= N_HEADS * i) & (lane < N_HEADS * (i + 1)), p, aug)
    aug_ref[0] = aug

    tn = (((0,), (1,)), ((), ()))
    qT = lax.dot_general(w_ref[:, 0:D], h, tn, preferred_element_type=F32).reshape(N_HEADS, DH, TM)
    qT = qT * lax.rsqrt(jnp.mean(qT * qT, axis=1, keepdims=True) + EPS)
    qT = qT.reshape(D, TM).astype(BF16)
    vT = lax.dot_general(w_ref[:, 2 * D:3 * D], h, tn, preferred_element_type=F32).astype(BF16)
    for j in range(TM // TQ):
        qT_ref[0, j] = qT[:, j * TQ:(j + 1) * TQ]
    for j in range(TM // TK):
        vT_ref[0, j] = vT[:, j * TK:(j + 1) * TK]

    k = _dot(h, w_ref[:, D:2 * D])
    kk = (k * k).astype(BF16)
    ss = jnp.concatenate(
        [_dot(kk[:, c * MXU:(c + 1) * MXU], gsum_ref[...]) for c in range(D // MXU)], axis=1)
    k2_ref[0] = (k * lax.rsqrt(ss * (1.0 / DH) + EPS) * gqk_ref[...]).astype(BF16)


def _fox_consts():
    g = np.kron(np.eye(MXU // DH), np.ones((DH, DH)))
    tri = np.tril(np.ones((TM, TM)))
    return jnp.asarray(g, BF16), jnp.asarray(tri, BF16)


def _fox_proj(x, mod, nmix, w_in, b_f, q_g, k_g):
    bsz, seq, _ = x.shape
    gsum, tri = _fox_consts()
    n_f = N_SPLIT * N_HEADS
    wf = jnp.zeros((D, LANES), F32).at[:, :n_f].set(jnp.tile(w_in[:, 3 * D:], (1, N_SPLIT))).astype(BF16)
    bf = jnp.zeros((1, LANES), F32).at[0, :n_f].set(jnp.tile(b_f, N_SPLIT))
    gqk = jnp.tile(q_g * k_g * (LOG2E * DH ** -0.5), N_HEADS).reshape(1, D)
    tok = pl.BlockSpec((1, TM, D), lambda b, s: (b, s, 0))
    return pl.pallas_call(
        _fox_proj_kernel,
        grid=(bsz, seq // TM),
        in_specs=[
            tok,
            pl.BlockSpec((1, 6, D), lambda b, s: (b, 0, 0)),
            _const_spec((1, D)),
            _const_spec(w_in.shape), _const_spec((D, LANES)),
            _const_spec((1, LANES)), _const_spec((1, D)), _const_spec((MXU, MXU)),
            _const_spec((TM, TM)),
        ],
        out_specs=[
            pl.BlockSpec((1, TM // TQ, D, TQ), lambda b, s: (b, s, 0, 0)),
            pl.BlockSpec((1, TM // TK, D, TK), lambda b, s: (b, s, 0, 0)),
            tok,
            pl.BlockSpec((1, TM, LANES), lambda b, s: (b, s, 0)),
        ],
        out_shape=[
            jax.ShapeDtypeStruct((bsz, seq // TQ, D, TQ), BF16),
            jax.ShapeDtypeStruct((bsz, seq // TK, D, TK), BF16),
            jax.ShapeDtypeStruct((bsz, seq, D), BF16),
            jax.ShapeDtypeStruct((bsz, seq, LANES), BF16),
        ],
        scratch_shapes=[pltpu.VMEM((8, LANES), F32)],
        compiler_params=_params(),
        name="fox_proj",
    )(x, mod, nmix.reshape(1, D), w_in.astype(BF16), wf, bf, gqk, gsum, tri)


def _fox_attn_kernel(qT_ref, k_ref, aug_ref, vT_ref, o_ref, qpad_ref, sa_ref, sb_ref, m_ref, acc_ref):
    head0 = pl.program_id(1) * HPS
    row = lax.broadcasted_iota(jnp.int32, (LANES, TQ), 0)
    zeros = jnp.zeros((DH, TQ), BF16)
    ones_rows = jnp.ones((16, TK), BF16)
    key_le_query = (lax.broadcasted_iota(jnp.int32, (TK, TQ), 0)
                    <= lax.broadcasted_iota(jnp.int32, (TK, TQ), 1))

    def build_qpad(qi):
        for hd in range(HPS):
            pick = row == head0 + hd
            for i in range(1, N_SPLIT):
                pick = pick | (row == head0 + hd + N_HEADS * i)
            ones = jnp.where(pick, 1.0, 0.0).astype(BF16)
            q = qT_ref[0, qi, hd * DH:(hd + 1) * DH, :]
            parts = [q, zeros] if hd % 2 == 0 else [zeros, q]
            qpad_ref[(qi % 2) * HPS + hd] = jnp.concatenate(parts + [ones], axis=0)

    def init():
        m_ref[...] = jnp.full(m_ref.shape, -1e30, F32)
        acc_ref[...] = jnp.zeros(acc_ref.shape, F32)

    def finish(qi):
        oT = jnp.concatenate(
            [acc_ref[hd, :DH, :] / acc_ref[hd, DH:DH + 1, :] for hd in range(HPS)], axis=0)
        o_ref[0, pl.ds(pl.multiple_of(qi * TQ, TQ), TQ), :] = oT.T.astype(BF16)
        init()

    def scores(t, s_ref, q):
        k0 = pl.multiple_of(t * TK, TK)
        aug = aug_ref[0, pl.ds(k0, TK), :]
        for hd in range(HPS):
            pair = slice((hd // 2) * LANES, (hd // 2 + 1) * LANES)
            kk = jnp.concatenate([k_ref[0, pl.ds(k0, TK), pair], aug], axis=1)
            s_ref[hd] = _dot(kk, qpad_ref[(q % 2) * HPS + hd])

    def update(t, s_ref, masked):
        for hd in range(HPS):
            sT = s_ref[hd]
            if masked:
                sT = jnp.where(key_le_query, sT, -jnp.inf)
            m_old = m_ref[hd]
            m_new = jnp.maximum(m_old, jnp.max(sT, axis=0, keepdims=True))
            p = jnp.exp2(sT - m_new).astype(BF16)
            va = jnp.concatenate([vT_ref[0, t, hd * DH:(hd + 1) * DH, :], ones_rows], axis=0)
            acc_ref[hd] = acc_ref[hd] * jnp.exp2(m_old - m_new) + _dot(va, p)
            m_ref[hd] = m_new

    def tiles(base, count, q):
        for i in range(0, count, 2):
            scores(base + i + 1, sb_ref, q)
            update(base + i, sa_ref, False)
            scores(base + i + 2, sa_ref, q)
            update(base + i + 1, sb_ref, False)

    n_q = qT_ref.shape[1]

    def query_tile(qi, _):
        nxt = jnp.minimum(qi + 1, n_q - 1)
        n_main = qi // KT_UNROLL

        def main(j, _):
            tiles(j * KT_UNROLL, KT_UNROLL, qi)
            return 0

        lax.fori_loop(0, n_main, main, 0)
        left = qi - n_main * KT_UNROLL

        def pair(j, _):
            tiles(n_main * KT_UNROLL + 2 * j, 2, qi)
            return 0

        lax.fori_loop(0, left // 2, pair, 0)

        @pl.when(qi % 2 == 0)
        def _():
            update(qi, sa_ref, True)
            build_qpad(nxt)
            scores(0, sa_ref, nxt)
            finish(qi)

        @pl.when(qi % 2 == 1)
        def _():
            scores(qi, sb_ref, qi)
            update(qi - 1, sa_ref, False)
            build_qpad(nxt)
            scores(0, sa_ref, nxt)
            update(qi, sb_ref, True)
            finish(qi)

        return 0

    build_qpad(0)
    init()
    scores(0, sa_ref, 0)
    lax.fori_loop(0, n_q, query_tile, 0)


def _fox_attn(qT, k2, aug, vT):
    bsz, nq, _, _ = qT.shape
    seq = k2.shape[1]
    assert TQ == TK
    gw = HPS * DH
    return pl.pallas_call(
        _fox_attn_kernel,
        grid=(bsz, N_HEADS // HPS),
        in_specs=[
            pl.BlockSpec((1, nq, gw, TQ), lambda b, g: (b, 0, g, 0)),
            pl.BlockSpec((1, seq, gw), lambda b, g: (b, 0, g)),
            pl.BlockSpec((1, seq, LANES), lambda b, g: (b, 0, 0)),
            pl.BlockSpec((1, seq // TK, gw, TK), lambda b, g: (b, 0, g, 0)),
        ],
        out_specs=pl.BlockSpec((1, seq, gw), lambda b, g: (b, 0, g)),
        out_shape=jax.ShapeDtypeStruct((bsz, seq, D), BF16),
        scratch_shapes=[
            pltpu.VMEM((2 * HPS, MXU, TQ), BF16),
            pltpu.VMEM((HPS, TK, TQ), F32),
            pltpu.VMEM((HPS, TK, TQ), F32),
            pltpu.VMEM((HPS, 1, TQ), F32),
            pltpu.VMEM((HPS, DH + 16, TQ), F32),
        ],
        compiler_params=_params(),
        name="fox_attn",
    )(qT, k2, aug, vT)


def _fox_tail_kernel(x_ref, o_ref_in, mod_ref, nmlp_ref, wout_ref, w1_ref, w2_ref, out_ref):
    x = x_ref[0]
    mod = mod_ref[0]
    x1 = x + mod[2:3] * _dot(o_ref_in[0], wout_ref[...])
    out_ref[0] = _mlp_tail(x1, mod, nmlp_ref[...], w1_ref, w2_ref)


def _fox_tail(x, o, mod, nmlp, w_out, w1_all, w2_all, layer):
    bsz, seq, _ = x.shape
    tok = pl.BlockSpec((1, TM, D), lambda b, s: (b, s, 0))
    return pl.pallas_call(
        _fox_tail_kernel,
        grid=(bsz, seq // TM),
        in_specs=[
            tok, tok,
            pl.BlockSpec((1, 6, D), lambda b, s: (b, 0, 0)),
            _const_spec((1, D)),
            _const_spec((D, D)), _layer_spec(w1_all.shape, layer), _layer_spec(w2_all.shape, layer),
        ],
        out_specs=tok,
        out_shape=jax.ShapeDtypeStruct(x.shape, F32),
        compiler_params=_params(),
        name="fox_tail",
    )(x, o, mod, nmlp.reshape(1, D), w_out.astype(BF16), w1_all, w2_all)


def kernel(x, c, norm_mix, norm_mlp, w_ada, b_ada, w_mlp_in, w_mlp_out, fox_w_in, fox_b_f, fox_q_norm, fox_k_norm, fox_w_out, sg_w_in, sg_ln_g, sg_ln_b, sg_w_s, sg_b_s, sg_w_out, cv_w_pw1, cv_b_pw1, cv_w_dw, cv_b_dw, cv_ln_g, cv_ln_b, cv_w_pw2, cv_b_pw2):
    depth = w_ada.shape[0]
    assert x.shape[1] % TM == 0 and x.shape[2] == D
    mods = _ada(c, w_ada, b_ada)
    w1_all = w_mlp_in.astype(BF16)
    w2_all = w_mlp_out.astype(BF16)
    for i in range(depth):
        kind, j = i % 3, i // 3
        mod = mods[i]
        if kind == 0:
            qT, vT, k2, aug = _fox_proj(x, mod, norm_mix[i], fox_w_in[j], fox_b_f[j],
                                        fox_q_norm[j], fox_k_norm[j])
            o = _fox_attn(qT, k2, aug, vT)
            x = _fox_tail(x, o, mod, norm_mlp[i], fox_w_out[j], w1_all, w2_all, i)
        elif kind == 1:
            x = _sg_layer(x, mod, norm_mix[i], norm_mlp[i], sg_w_in[j], sg_ln_g[j], sg_ln_b[j],
                          sg_w_s[j], sg_b_s[j], sg_w_out[j], w1_all, w2_all, i)
        else:
            x = _cv_layer(x, mod, norm_mix[i], norm_mlp[i], cv_w_pw1[j], cv_b_pw1[j], cv_w_dw[j],
                          cv_b_dw[j], cv_ln_g[j], cv_ln_b[j], cv_w_pw2[j], cv_b_pw2[j],
                          w1_all[i], w2_all[i])
    return x
```

```python
import functools

import jax
import jax.numpy as jnp
import numpy as np
from jax import lax
from jax.experimental import pallas as pl
from jax.experimental.pallas import tpu as pltpu

F32 = jnp.float32
BF16 = jnp.bfloat16

D = 1024
N_HEADS = 16
DH = D // N_HEADS
D_FF = 4 * D
EPS = 1e-6
LOG2E = 1.4426950408889634
SG_CHUNK = 128
SG_BLOCK = 64
SG_GROUPS = 8
CONV_W = 31
HALO = 32

LANES = 128
MXU = 256
TM = 512
TQ = 256
TK = 256
HPS = 4
KT_UNROLL = 4
FF_CHUNK = 1024
CONV_ROWS = 128
BF16_ROWS = 16
ADA_TN = 3072
N_SPLIT = 3
VMEM_LIMIT = 56 * 1024 * 1024


def _const_spec(shape):
    nd = len(shape)
    return pl.BlockSpec(shape, lambda *_: (0,) * nd, pipeline_mode=pl.Buffered(1))


def _layer_spec(shape, layer):
    nd = len(shape)
    return pl.BlockSpec((1,) + tuple(shape[1:]), lambda *_: (layer,) + (0,) * (nd - 1),
                        pipeline_mode=pl.Buffered(1))


def _params():
    return pltpu.CompilerParams(
        dimension_semantics=("arbitrary", "arbitrary"), vmem_limit_bytes=VMEM_LIMIT)


def _dot(a, b):
    return jnp.dot(a, b, preferred_element_type=F32)


def _sigmoid(x):
    return 1.0 / (1.0 + jnp.exp(-x))


def _rms_mod(x, gain, scale, shift):
    ms = jnp.mean(x * x, axis=-1, keepdims=True)
    return (x * lax.rsqrt(ms + EPS) * gain) * (1.0 + scale) + shift


def _layer_norm(x, g, b):
    mu = jnp.mean(x, axis=-1, keepdims=True)
    xc = x - mu
    var = jnp.mean(xc * xc, axis=-1, keepdims=True)
    return xc * lax.rsqrt(var + EPS) * g + b


def _split_bf16(x):
    pieces = []
    r = x
    for _ in range(N_SPLIT):
        p = r.astype(BF16)
        pieces.append(p)
        r = r - p.astype(F32)
    return pieces


def _mlp_tail(x1, mod, nmlp, w1_ref, w2_ref):
    w1_ref, w2_ref = w1_ref.at[0], w2_ref.at[0]
    h = _rms_mod(x1, nmlp, mod[4:5], mod[3:4]).astype(BF16)
    acc = jnp.zeros(x1.shape, F32)
    for c in range(D_FF // FF_CHUNK):
        lo = c * FF_CHUNK
        hid = _dot(h, w1_ref[:, lo:lo + FF_CHUNK])
        hid = jnp.square(jnp.maximum(hid, 0.0)).astype(BF16)
        acc = acc + _dot(hid, w2_ref[lo:lo + FF_CHUNK, :])
    return x1 + mod[5:6] * acc


def _ada_kernel(c_ref, w_ref, b_ref, o_ref):
    c = c_ref[...]
    rows = c.shape[0]
    ca = _split_bf16(c * _sigmoid(c))
    w = w_ref[0]
    w_hi = w.astype(BF16)
    w_lo = (w - w_hi.astype(F32)).astype(BF16)
    r_hi = _dot(jnp.concatenate(ca, axis=0), w_hi)
    r_lo = _dot(jnp.concatenate(ca[:2], axis=0), w_lo)
    out = b_ref[0] + r_lo[0:rows] + r_lo[rows:2 * rows]
    for i in range(N_SPLIT):
        out = out + r_hi[i * rows:(i + 1) * rows]
    o_ref[0] = out


def _ada(c, w_ada, b_ada):
    depth, _, n = w_ada.shape
    bsz = c.shape[0]
    rows = BF16_ROWS
    tn = ADA_TN
    assert bsz <= rows and n % tn == 0
    cp = jnp.zeros((rows, D), F32).at[:bsz].set(c)
    out = pl.pallas_call(
        _ada_kernel,
        grid=(depth, n // tn),
        in_specs=[
            pl.BlockSpec((rows, D), lambda i, j: (0, 0)),
            pl.BlockSpec((1, D, tn), lambda i, j: (i, 0, j)),
            pl.BlockSpec((1, 1, tn), lambda i, j: (i, 0, j)),
        ],
        out_specs=pl.BlockSpec((1, rows, tn), lambda i, j: (i, 0, j)),
        out_shape=jax.ShapeDtypeStruct((depth, rows, n), F32),
        compiler_params=_params(),
        name="ada_mod",
    )(cp, w_ada, b_ada.reshape(depth, 1, n))
    return out[:, :bsz].reshape(depth, bsz, 6, D)


def _sg_kernel(x_ref, mod_ref, nmix_ref, nmlp_ref, win_ref, lng_ref, lnb_ref, ws_ref, bs_ref,
               wout_ref, w1_ref, w2_ref, o_ref, gated_ref):
    x = x_ref[0]
    mod = mod_ref[0]
    h = _rms_mod(x, nmix_ref[...], mod[1:2], mod[0:1]).astype(BF16)
    uv = jax.nn.gelu(_dot(h, win_ref[...]), approximate=True)
    u = uv[:, :D]
    v = _layer_norm(uv[:, D:], lng_ref[...], lnb_ref[...]).astype(BF16)

    t = lax.broadcasted_iota(jnp.int32, (SG_CHUNK, SG_CHUNK), 0) // SG_BLOCK
    s = lax.broadcasted_iota(jnp.int32, (SG_CHUNK, SG_CHUNK), 1) // SG_BLOCK
    causal = s <= t
    gd = D // SG_GROUPS
    for g in range(SG_GROUPS):
        ws = jnp.where(causal, ws_ref[g], 0.0).astype(BF16)
        bias = bs_ref[g]
        cols = slice(g * gd, (g + 1) * gd)
        for j in range(TM // (2 * SG_CHUNK)):
            r0 = j * 2 * SG_CHUNK
            r1 = r0 + SG_CHUNK
            r2 = r1 + SG_CHUNK
            rhs = jnp.concatenate([v[r0:r1, cols], v[r1:r2, cols]], axis=1)
            mix = _dot(ws, rhs)
            gated_ref[r0:r1, cols] = (u[r0:r1, cols] * (mix[:, :gd] + bias)).astype(BF16)
            gated_ref[r1:r2, cols] = (u[r1:r2, cols] * (mix[:, gd:] + bias)).astype(BF16)

    x1 = x + mod[2:3] * _dot(gated_ref[...], wout_ref[...])
    o_ref[0] = _mlp_tail(x1, mod, nmlp_ref[...], w1_ref, w2_ref)


def _sg_layer(x, mod, nmix, nmlp, w_in, ln_g, ln_b, w_s, b_s, w_out, w1_all, w2_all, layer):
    bsz, seq, _ = x.shape
    tok = pl.BlockSpec((1, TM, D), lambda b, s: (b, s, 0))
    bs_x = jnp.broadcast_to(b_s[:, :, None], (SG_GROUPS, SG_CHUNK, D // SG_GROUPS))
    return pl.pallas_call(
        _sg_kernel,
        grid=(bsz, seq // TM),
        in_specs=[
            tok,
            pl.BlockSpec((1, 6, D), lambda b, s: (b, 0, 0)),
            _const_spec((1, D)), _const_spec((1, D)),
            _const_spec((D, 2 * D)), _const_spec((1, D)), _const_spec((1, D)),
            _const_spec((SG_GROUPS, SG_CHUNK, SG_CHUNK)),
            _const_spec((SG_GROUPS, SG_CHUNK, D // SG_GROUPS)),
            _const_spec((D, D)), _layer_spec(w1_all.shape, layer), _layer_spec(w2_all.shape, layer),
        ],
        out_specs=tok,
        out_shape=jax.ShapeDtypeStruct(x.shape, F32),
        scratch_shapes=[pltpu.VMEM((TM, D), BF16)],
        compiler_params=_params(),
        name="sg_layer",
    )(x, mod, nmix.reshape(1, D), nmlp.reshape(1, D), w_in.astype(BF16),
      ln_g.reshape(1, D), ln_b.reshape(1, D), w_s, bs_x, w_out.astype(BF16), w1_all, w2_all)


def _cv_kernel(n_s, x_ref, moda_ref, modb_ref, nmix_ref, nmlp_ref, wpw1_ref, bpw1_ref, wdw_ref,
               bdw_ref, lng_ref, lnb_ref, wpw2_ref, bpw2_ref, w1_ref, w2_ref, o_ref,
               ybuf_ref, conv_ref, xprev_ref, x1_ref, h2_ref, acc_ref):
    i = pl.program_id(0)
    n_lt = D // LANES

    @pl.when(i == 0)
    def _():
        conv_ref[...] = jnp.zeros(conv_ref.shape, F32)
        xprev_ref[...] = jnp.zeros(xprev_ref.shape, F32)

    @pl.when(i % n_s == 0)
    def _():
        ybuf_ref[:, 0:HALO, :] = jnp.zeros((n_lt, HALO, LANES), F32)

    modb = modb_ref[0]
    conv = jnp.concatenate([conv_ref[c] for c in range(n_lt)], axis=1)
    z = _layer_norm(conv, lng_ref[...], lnb_ref[...])
    z = (z * _sigmoid(z)).astype(BF16)
    x1 = xprev_ref[...] + modb[2:3] * (_dot(z, wpw2_ref[...]) + bpw2_ref[...])
    x1_ref[...] = x1
    h2_ref[...] = _rms_mod(x1, nmlp_ref[...], modb[4:5], modb[3:4]).astype(BF16)
    acc_ref[...] = jnp.zeros(acc_ref.shape, F32)

    x = x_ref[0]
    moda = moda_ref[0]
    h = _rms_mod(x, nmix_ref[...], moda[1:2], moda[0:1]).astype(BF16)
    yz = _dot(h, wpw1_ref[...]) + bpw1_ref[...]
    y = yz[:, :D] * _sigmoid(yz[:, D:])
    for c in range(n_lt):
        ybuf_ref[c, HALO:HALO + TM, :] = y[:, c * LANES:(c + 1) * LANES]
    xprev_ref[...] = x

    first = HALO - (CONV_W - 1)
    rb = CONV_ROWS
    win_rows = rb + HALO

    def conv_block(r, c):
        win = ybuf_ref[c, r * rb:r * rb + win_rows, :]
        acc = jnp.zeros((rb, LANES), F32) + bdw_ref[c]
        for b in range(8):
            phase = win if b == 0 else pltpu.roll(win, win_rows - b, axis=0)
            for j in range(CONV_W):
                off = first + j
                if off % 8 == b:
                    acc = acc + wdw_ref[c, j:j + 1, :] * phase[off - b:off - b + rb, :]
        conv_ref[c, r * rb:(r + 1) * rb, :] = acc

    def chunk(c, _):
        hid = jnp.maximum(_dot(h2_ref[...], w1_ref[c]).astype(BF16), 0.0)
        acc_ref[...] += _dot(hid * hid, w2_ref[c])
        for r in range(TM // rb):
            conv_block(r, c)
        return 0

    lax.fori_loop(0, n_lt, chunk, 0)
    o_ref[0] = x1_ref[...] + modb[5:6] * acc_ref[...]
    ybuf_ref[:, 0:HALO, :] = ybuf_ref[:, TM:TM + HALO, :]


def _cv_layer(x, mod, nmix, nmlp, w_pw1, b_pw1, w_dw, b_dw, ln_g, ln_b, w_pw2, b_pw2, w1, w2):
    bsz, seq, _ = x.shape
    n_s = seq // TM
    n = bsz * n_s
    n_lt = D // LANES
    ffc = D_FF // n_lt

    def cur(i):
        return jnp.minimum(i, n - 1)

    def prev(i):
        return jnp.maximum(i - 1, 0)

    wdw = jnp.zeros((HALO, D), F32).at[:CONV_W].set(w_dw).reshape(HALO, n_lt, LANES).transpose(1, 0, 2)
    bdw = b_dw.reshape(n_lt, 1, LANES)
    w1c = w1.astype(BF16).reshape(D, n_lt, ffc).transpose(1, 0, 2)
    w2c = w2.astype(BF16).reshape(n_lt, ffc, D)
    return pl.pallas_call(
        functools.partial(_cv_kernel, n_s),
        grid=(n + 1,),
        in_specs=[
            pl.BlockSpec((1, TM, D), lambda i: (cur(i) // n_s, cur(i) % n_s, 0)),
            pl.BlockSpec((1, 6, D), lambda i: (cur(i) // n_s, 0, 0)),
            pl.BlockSpec((1, 6, D), lambda i: (prev(i) // n_s, 0, 0)),
            _const_spec((1, D)), _const_spec((1, D)),
            _const_spec((D, 2 * D)), _const_spec((1, 2 * D)),
            _const_spec((n_lt, HALO, LANES)), _const_spec((n_lt, 1, LANES)),
            _const_spec((1, D)), _const_spec((1, D)),
            _const_spec((D, D)), _const_spec((1, D)),
            _const_spec((n_lt, D, ffc)), _const_spec((n_lt, ffc, D)),
        ],
        out_specs=pl.BlockSpec((1, TM, D), lambda i: (prev(i) // n_s, prev(i) % n_s, 0)),
        out_shape=jax.ShapeDtypeStruct(x.shape, F32),
        scratch_shapes=[
            pltpu.VMEM((n_lt, TM + HALO, LANES), F32),
            pltpu.VMEM((n_lt, TM, LANES), F32),
            pltpu.VMEM((TM, D), F32),
            pltpu.VMEM((TM, D), F32),
            pltpu.VMEM((TM, D), BF16),
            pltpu.VMEM((TM, D), F32),
        ],
        compiler_params=pltpu.CompilerParams(
            dimension_semantics=("arbitrary",), vmem_limit_bytes=VMEM_LIMIT),
        name="cv_layer",
    )(x, mod, mod, nmix.reshape(1, D), nmlp.reshape(1, D), w_pw1.astype(BF16),
      b_pw1.reshape(1, 2 * D), wdw, bdw, ln_g.reshape(1, D), ln_b.reshape(1, D),
      w_pw2.astype(BF16), b_pw2.reshape(1, D), w1c, w2c)


def _fox_proj_kernel(x_ref, mod_ref, nmix_ref, w_ref, wf_ref, bf_ref, gqk_ref, gsum_ref,
                     tri_ref, qT_ref, vT_ref, k2_ref, aug_ref, carry_ref):
    @pl.when(pl.program_id(1) == 0)
    def _():
        carry_ref[...] = jnp.zeros(carry_ref.shape, F32)

    x = x_ref[0]
    mod = mod_ref[0]
    h = _rms_mod(x, nmix_ref[...], mod[1:2], mod[0:1]).astype(BF16)

    f_pre = _dot(h, wf_ref[...]) + bf_ref[...]
    log_f = jnp.minimum(f_pre, 0.0) - jnp.log1p(jnp.exp(-jnp.abs(f_pre)))
    cum = carry_ref[0:1, :]
    for p in _split_bf16(log_f):
        cum = cum + _dot(tri_ref[...], p)
    carry_ref[0:1, :] = cum[TM - 1:TM, :]

    lane = lax.broadcasted_iota(jnp.int32, (TM, LANES), 1)
    aug = jnp.zeros((TM, LANES), BF16)
    for i, p in enumerate(_split_bf16(cum * -LOG2E)):
        aug = jnp.where((lane >= N_HEADS * i) & (lane < N_HEADS * (i + 1)), p, aug)
    aug_ref[0] = aug

    tn = (((0,), (1,)), ((), ()))
    qT = lax.dot_general(w_ref[:, 0:D], h, tn, preferred_element_type=F32).reshape(N_HEADS, DH, TM)
    qT = qT * lax.rsqrt(jnp.mean(qT * qT, axis=1, keepdims=True) + EPS)
    qT = qT.reshape(D, TM).astype(BF16)
    vT = lax.dot_general(w_ref[:, 2 * D:3 * D], h, tn, preferred_element_type=F32).astype(BF16)
    for j in range(TM // TQ):
        qT_ref[0, j] = qT[:, j * TQ:(j + 1) * TQ]
    for j in range(TM // TK):
        vT_ref[0, j] = vT[:, j * TK:(j + 1) * TK]

    k = _dot(h, w_ref[:, D:2 * D])
    kk = (k * k).astype(BF16)
    ss = jnp.concatenate(
        [_dot(kk[:, c * MXU:(c + 1) * MXU], gsum_ref[...]) for c in range(D // MXU)], axis=1)
    k2_ref[0] = (k * lax.rsqrt(ss * (1.0 / DH) + EPS) * gqk_ref[...]).astype(BF16)


def _fox_consts():
    g = np.kron(np.eye(MXU // DH), np.ones((DH, DH)))
    tri = np.tril(np.ones((TM, TM)))
    return jnp.asarray(g, BF16), jnp.asarray(tri, BF16)


def _fox_proj(x, mod, nmix, w_in, b_f, q_g, k_g):
    bsz, seq, _ = x.shape
    gsum, tri = _fox_consts()
    n_f = N_SPLIT * N_HEADS
    wf = jnp.zeros((D, LANES), F32).at[:, :n_f].set(jnp.tile(w_in[:, 3 * D:], (1, N_SPLIT))).astype(BF16)
    bf = jnp.zeros((1, LANES), F32).at[0, :n_f].set(jnp.tile(b_f, N_SPLIT))
    gqk = jnp.tile(q_g * k_g * (LOG2E * DH ** -0.5), N_HEADS).reshape(1, D)
    tok = pl.BlockSpec((1, TM, D), lambda b, s: (b, s, 0))
    return pl.pallas_call(
        _fox_proj_kernel,
        grid=(bsz, seq // TM),
        in_specs=[
            tok,
            pl.BlockSpec((1, 6, D), lambda b, s: (b, 0, 0)),
            _const_spec((1, D)),
            _const_spec(w_in.shape), _const_spec((D, LANES)),
            _const_spec((1, LANES)), _const_spec((1, D)), _const_spec((MXU, MXU)),
            _const_spec((TM, TM)),
        ],
        out_specs=[
            pl.BlockSpec((1, TM // TQ, D, TQ), lambda b, s: (b, s, 0, 0)),
            pl.BlockSpec((1, TM // TK, D, TK), lambda b, s: (b, s, 0, 0)),
            tok,
            pl.BlockSpec((1, TM, LANES), lambda b, s: (b, s, 0)),
        ],
        out_shape=[
            jax.ShapeDtypeStruct((bsz, seq // TQ, D, TQ), BF16),
            jax.ShapeDtypeStruct((bsz, seq // TK, D, TK), BF16),
            jax.ShapeDtypeStruct((bsz, seq, D), BF16),
            jax.ShapeDtypeStruct((bsz, seq, LANES), BF16),
        ],
        scratch_shapes=[pltpu.VMEM((8, LANES), F32)],
        compiler_params=_params(),
        name="fox_proj",
    )(x, mod, nmix.reshape(1, D), w_in.astype(BF16), wf, bf, gqk, gsum, tri)


def _fox_attn_kernel(qT_ref, k_ref, aug_ref, vT_ref, o_ref, qpad_ref, sa_ref, sb_ref, m_ref, acc_ref):
    head0 = pl.program_id(1) * HPS
    row = lax.broadcasted_iota(jnp.int32, (LANES, TQ), 0)
    zeros = jnp.zeros((DH, TQ), BF16)
    ones_rows = jnp.ones((BF16_ROWS, TK), BF16)
    key_le_query = (lax.broadcasted_iota(jnp.int32, (TK, TQ), 0)
                    <= lax.broadcasted_iota(jnp.int32, (TK, TQ), 1))

    def build_qpad(qi):
        for hd in range(HPS):
            pick = row == head0 + hd
            for i in range(1, N_SPLIT):
                pick = pick | (row == head0 + hd + N_HEADS * i)
            ones = jnp.where(pick, 1.0, 0.0).astype(BF16)
            q = qT_ref[0, qi, hd * DH:(hd + 1) * DH, :]
            parts = [q, zeros] if hd % 2 == 0 else [zeros, q]
            qpad_ref[(qi % 2) * HPS + hd] = jnp.concatenate(parts + [ones], axis=0)

    def init():
        m_ref[...] = jnp.full(m_ref.shape, -1e30, F32)
        acc_ref[...] = jnp.zeros(acc_ref.shape, F32)

    def finish(qi):
        oT = jnp.concatenate(
            [acc_ref[hd, :DH, :] / acc_ref[hd, DH:DH + 1, :] for hd in range(HPS)], axis=0)
        o_ref[0, pl.ds(pl.multiple_of(qi * TQ, TQ), TQ), :] = oT.T.astype(BF16)
        init()

    def scores(t, s_ref, q):
        k0 = pl.multiple_of(t * TK, TK)
        aug = aug_ref[0, pl.ds(k0, TK), :]
        for hd in range(HPS):
            pair = slice((hd // 2) * LANES, (hd // 2 + 1) * LANES)
            kk = jnp.concatenate([k_ref[0, pl.ds(k0, TK), pair], aug], axis=1)
            s_ref[hd] = _dot(kk, qpad_ref[(q % 2) * HPS + hd])

    def update(t, s_ref, masked):
        for hd in range(HPS):
            sT = s_ref[hd]
            if masked:
                sT = jnp.where(key_le_query, sT, -jnp.inf)
            m_old = m_ref[hd]
            m_new = jnp.maximum(m_old, jnp.max(sT, axis=0, keepdims=True))
            p = jnp.exp2(sT - m_new).astype(BF16)
            va = jnp.concatenate([vT_ref[0, t, hd * DH:(hd + 1) * DH, :], ones_rows], axis=0)
            acc_ref[hd] = acc_ref[hd] * jnp.exp2(m_old - m_new) + _dot(va, p)
            m_ref[hd] = m_new

    def tiles(base, count, q):
        for i in range(0, count, 2):
            scores(base + i + 1, sb_ref, q)
            update(base + i, sa_ref, False)
            scores(base + i + 2, sa_ref, q)
            update(base + i + 1, sb_ref, False)

    n_q = qT_ref.shape[1]

    def query_tile(qi, _):
        nxt = jnp.minimum(qi + 1, n_q - 1)
        n_main = qi // KT_UNROLL

        def main(j, _):
            tiles(j * KT_UNROLL, KT_UNROLL, qi)
            return 0

        lax.fori_loop(0, n_main, main, 0)
        left = qi - n_main * KT_UNROLL

        def pair(j, _):
            tiles(n_main * KT_UNROLL + 2 * j, 2, qi)
            return 0

        lax.fori_loop(0, left // 2, pair, 0)

        @pl.when(qi % 2 == 0)
        def _():
            update(qi, sa_ref, True)
            build_qpad(nxt)
            scores(0, sa_ref, nxt)
            finish(qi)

        @pl.when(qi % 2 == 1)
        def _():
            scores(qi, sb_ref, qi)
            update(qi - 1, sa_ref, False)
            build_qpad(nxt)
            scores(0, sa_ref, nxt)
            update(qi, sb_ref, True)
            finish(qi)

        return 0

    build_qpad(0)
    init()
    scores(0, sa_ref, 0)
    lax.fori_loop(0, n_q, query_tile, 0)


def _fox_attn(qT, k2, aug, vT):
    bsz, nq, _, _ = qT.shape
    seq = k2.shape[1]
    assert TQ == TK
    gw = HPS * DH
    return pl.pallas_call(
        _fox_attn_kernel,
        grid=(bsz, N_HEADS // HPS),
        in_specs=[
            pl.BlockSpec((1, nq, gw, TQ), lambda b, g: (b, 0, g, 0)),
            pl.BlockSpec((1, seq, gw), lambda b, g: (b, 0, g)),
            pl.BlockSpec((1, seq, LANES), lambda b, g: (b, 0, 0)),
            pl.BlockSpec((1, seq // TK, gw, TK), lambda b, g: (b, 0, g, 0)),
        ],
        out_specs=pl.BlockSpec((1, seq, gw), lambda b, g: (b, 0, g)),
        out_shape=jax.ShapeDtypeStruct((bsz, seq, D), BF16),
        scratch_shapes=[
            pltpu.VMEM((2 * HPS, MXU, TQ), BF16),
            pltpu.VMEM((HPS, TK, TQ), F32),
            pltpu.VMEM((HPS, TK, TQ), F32),
            pltpu.VMEM((HPS, 1, TQ), F32),
            pltpu.VMEM((HPS, DH + BF16_ROWS, TQ), F32),
        ],
        compiler_params=_params(),
        name="fox_attn",
    )(qT, k2, aug, vT)


def _fox_tail_kernel(x_ref, o_ref_in, mod_ref, nmlp_ref, wout_ref, w1_ref, w2_ref, out_ref):
    x = x_ref[0]
    mod = mod_ref[0]
    x1 = x + mod[2:3] * _dot(o_ref_in[0], wout_ref[...])
    out_ref[0] = _mlp_tail(x1, mod, nmlp_ref[...], w1_ref, w2_ref)


def _fox_tail(x, o, mod, nmlp, w_out, w1_all, w2_all, layer):
    bsz, seq, _ = x.shape
    tok = pl.BlockSpec((1, TM, D), lambda b, s: (b, s, 0))
    return pl.pallas_call(
        _fox_tail_kernel,
        grid=(bsz, seq // TM),
        in_specs=[
            tok, tok,
            pl.BlockSpec((1, 6, D), lambda b, s: (b, 0, 0)),
            _const_spec((1, D)),
            _const_spec((D, D)), _layer_spec(w1_all.shape, layer), _layer_spec(w2_all.shape, layer),
        ],
        out_specs=tok,
        out_shape=jax.ShapeDtypeStruct(x.shape, F32),
        compiler_params=_params(),
        name="fox_tail",
    )(x, o, mod, nmlp.reshape(1, D), w_out.astype(BF16), w1_all, w2_all)


def kernel(x, c, norm_mix, norm_mlp, w_ada, b_ada, w_mlp_in, w_mlp_out, fox_w_in, fox_b_f, fox_q_norm, fox_k_norm, fox_w_out, sg_w_in, sg_ln_g, sg_ln_b, sg_w_s, sg_b_s, sg_w_out, cv_w_pw1, cv_b_pw1, cv_w_dw, cv_b_dw, cv_ln_g, cv_ln_b, cv_w_pw2, cv_b_pw2):
    depth = w_ada.shape[0]
    assert x.shape[1] % TM == 0 and x.shape[2] == D
    mods = _ada(c, w_ada, b_ada)
    w1_all = w_mlp_in.astype(BF16)
    w2_all = w_mlp_out.astype(BF16)
    for i in range(depth):
        kind, j = i % 3, i // 3
        mod = mods[i]
        if kind == 0:
            qT, vT, k2, aug = _fox_proj(x, mod, norm_mix[i], fox_w_in[j], fox_b_f[j],
                                        fox_q_norm[j], fox_k_norm[j])
            o = _fox_attn(qT, k2, aug, vT)
            x = _fox_tail(x, o, mod, norm_mlp[i], fox_w_out[j], w1_all, w2_all, i)
        elif kind == 1:
            x = _sg_layer(x, mod, norm_mix[i], norm_mlp[i], sg_w_in[j], sg_ln_g[j], sg_ln_b[j],
                          sg_w_s[j], sg_b_s[j], sg_w_out[j], w1_all, w2_all, i)
        else:
            x = _cv_layer(x, mod, norm_mix[i], norm_mlp[i], cv_w_pw1[j], cv_b_pw1[j], cv_w_dw[j],
                          cv_b_dw[j], cv_ln_g[j], cv_ln_b[j], cv_w_pw2[j], cv_b_pw2[j],
                          w1_all[i], w2_all[i])
    return x
```

```python
import functools

import jax
import jax.numpy as jnp
import numpy as np
from jax import lax
from jax.experimental import pallas as pl
from jax.experimental.pallas import tpu as pltpu

F32 = jnp.float32
BF16 = jnp.bfloat16

D = 1024
N_HEADS = 16
DH = D // N_HEADS
D_FF = 4 * D
EPS = 1e-6
LOG2E = 1.4426950408889634
SG_CHUNK = 128
SG_BLOCK = 64
SG_GROUPS = 8
CONV_W = 31
HALO = 32

LANES = 128
MXU = 256
TM = 512
TQ = 256
TK = 256
HPS = 8
KT_UNROLL = 4
FF_CHUNK = 1024
CONV_ROWS = 128
BF16_ROWS = 16
ADA_TN = 3072
N_SPLIT = 3
VMEM_LIMIT = 56 * 1024 * 1024


def _const_spec(shape):
    nd = len(shape)
    return pl.BlockSpec(shape, lambda *_: (0,) * nd, pipeline_mode=pl.Buffered(1))


def _layer_spec(shape, layer):
    nd = len(shape)
    return pl.BlockSpec((1,) + tuple(shape[1:]), lambda *_: (layer,) + (0,) * (nd - 1),
                        pipeline_mode=pl.Buffered(1))


def _params():
    return pltpu.CompilerParams(
        dimension_semantics=("arbitrary", "arbitrary"), vmem_limit_bytes=VMEM_LIMIT)


def _dot(a, b):
    return jnp.dot(a, b, preferred_element_type=F32)


def _sigmoid(x):
    return 1.0 / (1.0 + jnp.exp(-x))


def _rms_mod(x, gain, scale, shift):
    ms = jnp.mean(x * x, axis=-1, keepdims=True)
    return (x * lax.rsqrt(ms + EPS) * gain) * (1.0 + scale) + shift


def _layer_norm(x, g, b):
    mu = jnp.mean(x, axis=-1, keepdims=True)
    xc = x - mu
    var = jnp.mean(xc * xc, axis=-1, keepdims=True)
    return xc * lax.rsqrt(var + EPS) * g + b


def _split_bf16(x):
    pieces = []
    r = x
    for _ in range(N_SPLIT):
        p = r.astype(BF16)
        pieces.append(p)
        r = r - p.astype(F32)
    return pieces


def _mlp_tail(x1, mod, nmlp, w1_ref, w2_ref):
    w1_ref, w2_ref = w1_ref.at[0], w2_ref.at[0]
    h = _rms_mod(x1, nmlp, mod[4:5], mod[3:4]).astype(BF16)
    acc = jnp.zeros(x1.shape, F32)
    for c in range(D_FF // FF_CHUNK):
        lo = c * FF_CHUNK
        hid = _dot(h, w1_ref[:, lo:lo + FF_CHUNK])
        hid = jnp.square(jnp.maximum(hid, 0.0)).astype(BF16)
        acc = acc + _dot(hid, w2_ref[lo:lo + FF_CHUNK, :])
    return x1 + mod[5:6] * acc


def _ada_kernel(c_ref, w_ref, b_ref, o_ref):
    c = c_ref[...]
    rows = c.shape[0]
    ca = _split_bf16(c * _sigmoid(c))
    w = w_ref[0]
    w_hi = w.astype(BF16)
    w_lo = (w - w_hi.astype(F32)).astype(BF16)
    r_hi = _dot(jnp.concatenate(ca, axis=0), w_hi)
    r_lo = _dot(jnp.concatenate(ca[:2], axis=0), w_lo)
    out = b_ref[0] + r_lo[0:rows] + r_lo[rows:2 * rows]
    for i in range(N_SPLIT):
        out = out + r_hi[i * rows:(i + 1) * rows]
    o_ref[0] = out


def _ada(c, w_ada, b_ada):
    depth, _, n = w_ada.shape
    bsz = c.shape[0]
    rows = BF16_ROWS
    tn = ADA_TN
    assert bsz <= rows and n % tn == 0
    cp = jnp.zeros((rows, D), F32).at[:bsz].set(c)
    out = pl.pallas_call(
        _ada_kernel,
        grid=(depth, n // tn),
        in_specs=[
            pl.BlockSpec((rows, D), lambda i, j: (0, 0)),
            pl.BlockSpec((1, D, tn), lambda i, j: (i, 0, j)),
            pl.BlockSpec((1, 1, tn), lambda i, j: (i, 0, j)),
        ],
        out_specs=pl.BlockSpec((1, rows, tn), lambda i, j: (i, 0, j)),
        out_shape=jax.ShapeDtypeStruct((depth, rows, n), F32),
        compiler_params=_params(),
        name="ada_mod",
    )(cp, w_ada, b_ada.reshape(depth, 1, n))
    return out[:, :bsz].reshape(depth, bsz, 6, D)


def _sg_kernel(x_ref, mod_ref, nmix_ref, nmlp_ref, win_ref, lng_ref, lnb_ref, ws_ref, bs_ref,
               wout_ref, w1_ref, w2_ref, o_ref, gated_ref):
    x = x_ref[0]
    mod = mod_ref[0]
    h = _rms_mod(x, nmix_ref[...], mod[1:2], mod[0:1]).astype(BF16)
    uv = jax.nn.gelu(_dot(h, win_ref[...]), approximate=True)
    u = uv[:, :D]
    v = _layer_norm(uv[:, D:], lng_ref[...], lnb_ref[...]).astype(BF16)

    t = lax.broadcasted_iota(jnp.int32, (SG_CHUNK, SG_CHUNK), 0) // SG_BLOCK
    s = lax.broadcasted_iota(jnp.int32, (SG_CHUNK, SG_CHUNK), 1) // SG_BLOCK
    causal = s <= t
    gd = D // SG_GROUPS
    for g in range(SG_GROUPS):
        ws = jnp.where(causal, ws_ref[g], 0.0).astype(BF16)
        bias = bs_ref[g]
        cols = slice(g * gd, (g + 1) * gd)
        for j in range(TM // (2 * SG_CHUNK)):
            r0 = j * 2 * SG_CHUNK
            r1 = r0 + SG_CHUNK
            r2 = r1 + SG_CHUNK
            rhs = jnp.concatenate([v[r0:r1, cols], v[r1:r2, cols]], axis=1)
            mix = _dot(ws, rhs)
            gated_ref[r0:r1, cols] = (u[r0:r1, cols] * (mix[:, :gd] + bias)).astype(BF16)
            gated_ref[r1:r2, cols] = (u[r1:r2, cols] * (mix[:, gd:] + bias)).astype(BF16)

    x1 = x + mod[2:3] * _dot(gated_ref[...], wout_ref[...])
    o_ref[0] = _mlp_tail(x1, mod, nmlp_ref[...], w1_ref, w2_ref)


def _sg_layer(x, mod, nmix, nmlp, w_in, ln_g, ln_b, w_s, b_s, w_out, w1_all, w2_all, layer):
    bsz, seq, _ = x.shape
    tok = pl.BlockSpec((1, TM, D), lambda b, s: (b, s, 0))
    bs_x = jnp.broadcast_to(b_s[:, :, None], (SG_GROUPS, SG_CHUNK, D // SG_GROUPS))
    return pl.pallas_call(
        _sg_kernel,
        grid=(bsz, seq // TM),
        in_specs=[
            tok,
            pl.BlockSpec((1, 6, D), lambda b, s: (b, 0, 0)),
            _const_spec((1, D)), _const_spec((1, D)),
            _const_spec((D, 2 * D)), _const_spec((1, D)), _const_spec((1, D)),
            _const_spec((SG_GROUPS, SG_CHUNK, SG_CHUNK)),
            _const_spec((SG_GROUPS, SG_CHUNK, D // SG_GROUPS)),
            _const_spec((D, D)), _layer_spec(w1_all.shape, layer), _layer_spec(w2_all.shape, layer),
        ],
        out_specs=tok,
        out_shape=jax.ShapeDtypeStruct(x.shape, F32),
        scratch_shapes=[pltpu.VMEM((TM, D), BF16)],
        compiler_params=_params(),
        name="sg_layer",
    )(x, mod, nmix.reshape(1, D), nmlp.reshape(1, D), w_in.astype(BF16),
      ln_g.reshape(1, D), ln_b.reshape(1, D), w_s, bs_x, w_out.astype(BF16), w1_all, w2_all)


def _cv_kernel(n_s, x_ref, moda_ref, modb_ref, nmix_ref, nmlp_ref, wpw1_ref, bpw1_ref, wdw_ref,
               bdw_ref, lng_ref, lnb_ref, wpw2_ref, bpw2_ref, w1_ref, w2_ref, o_ref,
               ybuf_ref, conv_ref, xprev_ref, x1_ref, h2_ref, acc_ref):
    i = pl.program_id(0)
    n_lt = D // LANES

    @pl.when(i == 0)
    def _():
        conv_ref[...] = jnp.zeros(conv_ref.shape, F32)
        xprev_ref[...] = jnp.zeros(xprev_ref.shape, F32)

    @pl.when(i % n_s == 0)
    def _():
        ybuf_ref[:, 0:HALO, :] = jnp.zeros((n_lt, HALO, LANES), F32)

    modb = modb_ref[0]
    conv = jnp.concatenate([conv_ref[c] for c in range(n_lt)], axis=1)
    z = _layer_norm(conv, lng_ref[...], lnb_ref[...])
    z = (z * _sigmoid(z)).astype(BF16)
    x1 = xprev_ref[...] + modb[2:3] * (_dot(z, wpw2_ref[...]) + bpw2_ref[...])
    x1_ref[...] = x1
    h2_ref[...] = _rms_mod(x1, nmlp_ref[...], modb[4:5], modb[3:4]).astype(BF16)
    acc_ref[...] = jnp.zeros(acc_ref.shape, F32)

    x = x_ref[0]
    moda = moda_ref[0]
    h = _rms_mod(x, nmix_ref[...], moda[1:2], moda[0:1]).astype(BF16)
    yz = _dot(h, wpw1_ref[...]) + bpw1_ref[...]
    y = yz[:, :D] * _sigmoid(yz[:, D:])
    for c in range(n_lt):
        ybuf_ref[c, HALO:HALO + TM, :] = y[:, c * LANES:(c + 1) * LANES]
    xprev_ref[...] = x

    first = HALO - (CONV_W - 1)
    rb = CONV_ROWS
    win_rows = rb + HALO

    def conv_block(r, c):
        win = ybuf_ref[c, r * rb:r * rb + win_rows, :]
        acc = jnp.zeros((rb, LANES), F32) + bdw_ref[c]
        for b in range(8):
            phase = win if b == 0 else pltpu.roll(win, win_rows - b, axis=0)
            for j in range(CONV_W):
                off = first + j
                if off % 8 == b:
                    acc = acc + wdw_ref[c, j:j + 1, :] * phase[off - b:off - b + rb, :]
        conv_ref[c, r * rb:(r + 1) * rb, :] = acc

    def chunk(c, _):
        hid = jnp.maximum(_dot(h2_ref[...], w1_ref[c]).astype(BF16), 0.0)
        acc_ref[...] += _dot(hid * hid, w2_ref[c])
        for r in range(TM // rb):
            conv_block(r, c)
        return 0

    lax.fori_loop(0, n_lt, chunk, 0)
    o_ref[0] = x1_ref[...] + modb[5:6] * acc_ref[...]
    ybuf_ref[:, 0:HALO, :] = ybuf_ref[:, TM:TM + HALO, :]


def _cv_layer(x, mod, nmix, nmlp, w_pw1, b_pw1, w_dw, b_dw, ln_g, ln_b, w_pw2, b_pw2, w1, w2):
    bsz, seq, _ = x.shape
    n_s = seq // TM
    n = bsz * n_s
    n_lt = D // LANES
    ffc = D_FF // n_lt

    def cur(i):
        return jnp.minimum(i, n - 1)

    def prev(i):
        return jnp.maximum(i - 1, 0)

    wdw = jnp.zeros((HALO, D), F32).at[:CONV_W].set(w_dw).reshape(HALO, n_lt, LANES).transpose(1, 0, 2)
    bdw = b_dw.reshape(n_lt, 1, LANES)
    w1c = w1.astype(BF16).reshape(D, n_lt, ffc).transpose(1, 0, 2)
    w2c = w2.astype(BF16).reshape(n_lt, ffc, D)
    return pl.pallas_call(
        functools.partial(_cv_kernel, n_s),
        grid=(n + 1,),
        in_specs=[
            pl.BlockSpec((1, TM, D), lambda i: (cur(i) // n_s, cur(i) % n_s, 0)),
            pl.BlockSpec((1, 6, D), lambda i: (cur(i) // n_s, 0, 0)),
            pl.BlockSpec((1, 6, D), lambda i: (prev(i) // n_s, 0, 0)),
            _const_spec((1, D)), _const_spec((1, D)),
            _const_spec((D, 2 * D)), _const_spec((1, 2 * D)),
            _const_spec((n_lt, HALO, LANES)), _const_spec((n_lt, 1, LANES)),
            _const_spec((1, D)), _const_spec((1, D)),
            _const_spec((D, D)), _const_spec((1, D)),
            _const_spec((n_lt, D, ffc)), _const_spec((n_lt, ffc, D)),
        ],
        out_specs=pl.BlockSpec((1, TM, D), lambda i: (prev(i) // n_s, prev(i) % n_s, 0)),
        out_shape=jax.ShapeDtypeStruct(x.shape, F32),
        scratch_shapes=[
            pltpu.VMEM((n_lt, TM + HALO, LANES), F32),
            pltpu.VMEM((n_lt, TM, LANES), F32),
            pltpu.VMEM((TM, D), F32),
            pltpu.VMEM((TM, D), F32),
            pltpu.VMEM((TM, D), BF16),
            pltpu.VMEM((TM, D), F32),
        ],
        compiler_params=pltpu.CompilerParams(
            dimension_semantics=("arbitrary",), vmem_limit_bytes=VMEM_LIMIT),
        name="cv_layer",
    )(x, mod, mod, nmix.reshape(1, D), nmlp.reshape(1, D), w_pw1.astype(BF16),
      b_pw1.reshape(1, 2 * D), wdw, bdw, ln_g.reshape(1, D), ln_b.reshape(1, D),
      w_pw2.astype(BF16), b_pw2.reshape(1, D), w1c, w2c)


def _fox_proj_kernel(x_ref, mod_ref, nmix_ref, w_ref, wf_ref, bf_ref, gqk_ref, gsum_ref,
                     tri_ref, qT_ref, vT_ref, k2_ref, aug_ref, carry_ref):
    @pl.when(pl.program_id(1) == 0)
    def _():
        carry_ref[...] = jnp.zeros(carry_ref.shape, F32)

    x = x_ref[0]
    mod = mod_ref[0]
    h = _rms_mod(x, nmix_ref[...], mod[1:2], mod[0:1]).astype(BF16)

    f_pre = _dot(h, wf_ref[...]) + bf_ref[...]
    log_f = jnp.minimum(f_pre, 0.0) - jnp.log1p(jnp.exp(-jnp.abs(f_pre)))
    cum = carry_ref[0:1, :]
    for p in _split_bf16(log_f):
        cum = cum + _dot(tri_ref[...], p)
    carry_ref[0:1, :] = cum[TM - 1:TM, :]

    lane = lax.broadcasted_iota(jnp.int32, (TM, LANES), 1)
    aug = jnp.zeros((TM, LANES), BF16)
    for i, p in enumerate(_split_bf16(cum * -LOG2E)):
        aug = jnp.where((lane >= N_HEADS * i) & (lane < N_HEADS * (i + 1)), p, aug)
    aug_ref[0] = aug

    tn = (((0,), (1,)), ((), ()))
    qT = lax.dot_general(w_ref[:, 0:D], h, tn, preferred_element_type=F32).reshape(N_HEADS, DH, TM)
    qT = qT * lax.rsqrt(jnp.mean(qT * qT, axis=1, keepdims=True) + EPS)
    qT = qT.reshape(D, TM).astype(BF16)
    vT = lax.dot_general(w_ref[:, 2 * D:3 * D], h, tn, preferred_element_type=F32).astype(BF16)
    for j in range(TM // TQ):
        qT_ref[0, j] = qT[:, j * TQ:(j + 1) * TQ]
    for j in range(TM // TK):
        vT_ref[0, j] = vT[:, j * TK:(j + 1) * TK]

    k = _dot(h, w_ref[:, D:2 * D])
    kk = (k * k).astype(BF16)
    ss = jnp.concatenate(
        [_dot(kk[:, c * MXU:(c + 1) * MXU], gsum_ref[...]) for c in range(D // MXU)], axis=1)
    k2_ref[0] = (k * lax.rsqrt(ss * (1.0 / DH) + EPS) * gqk_ref[...]).astype(BF16)


def _fox_consts():
    g = np.kron(np.eye(MXU // DH), np.ones((DH, DH)))
    tri = np.tril(np.ones((TM, TM)))
    return jnp.asarray(g, BF16), jnp.asarray(tri, BF16)


def _fox_proj(x, mod, nmix, w_in, b_f, q_g, k_g):
    bsz, seq, _ = x.shape
    gsum, tri = _fox_consts()
    n_f = N_SPLIT * N_HEADS
    wf = jnp.zeros((D, LANES), F32).at[:, :n_f].set(jnp.tile(w_in[:, 3 * D:], (1, N_SPLIT))).astype(BF16)
    bf = jnp.zeros((1, LANES), F32).at[0, :n_f].set(jnp.tile(b_f, N_SPLIT))
    gqk = jnp.tile(q_g * k_g * (LOG2E * DH ** -0.5), N_HEADS).reshape(1, D)
    tok = pl.BlockSpec((1, TM, D), lambda b, s: (b, s, 0))
    return pl.pallas_call(
        _fox_proj_kernel,
        grid=(bsz, seq // TM),
        in_specs=[
            tok,
            pl.BlockSpec((1, 6, D), lambda b, s: (b, 0, 0)),
            _const_spec((1, D)),
            _const_spec(w_in.shape), _const_spec((D, LANES)),
            _const_spec((1, LANES)), _const_spec((1, D)), _const_spec((MXU, MXU)),
            _const_spec((TM, TM)),
        ],
        out_specs=[
            pl.BlockSpec((1, TM // TQ, D, TQ), lambda b, s: (b, s, 0, 0)),
            pl.BlockSpec((1, TM // TK, D, TK), lambda b, s: (b, s, 0, 0)),
            tok,
            pl.BlockSpec((1, TM, LANES), lambda b, s: (b, s, 0)),
        ],
        out_shape=[
            jax.ShapeDtypeStruct((bsz, seq // TQ, D, TQ), BF16),
            jax.ShapeDtypeStruct((bsz, seq // TK, D, TK), BF16),
            jax.ShapeDtypeStruct((bsz, seq, D), BF16),
            jax.ShapeDtypeStruct((bsz, seq, LANES), BF16),
        ],
        scratch_shapes=[pltpu.VMEM((8, LANES), F32)],
        compiler_params=_params(),
        name="fox_proj",
    )(x, mod, nmix.reshape(1, D), w_in.astype(BF16), wf, bf, gqk, gsum, tri)


def _fox_attn_kernel(qT_ref, k_ref, aug_ref, vT_ref, o_ref, qpad_ref, sa_ref, sb_ref, m_ref, acc_ref):
    head0 = pl.program_id(1) * HPS
    row = lax.broadcasted_iota(jnp.int32, (LANES, TQ), 0)
    zeros = jnp.zeros((DH, TQ), BF16)
    ones_rows = jnp.ones((BF16_ROWS, TK), BF16)
    key_le_query = (lax.broadcasted_iota(jnp.int32, (TK, TQ), 0)
                    <= lax.broadcasted_iota(jnp.int32, (TK, TQ), 1))

    def build_qpad(qi):
        for hd in range(HPS):
            pick = row == head0 + hd
            for i in range(1, N_SPLIT):
                pick = pick | (row == head0 + hd + N_HEADS * i)
            ones = jnp.where(pick, 1.0, 0.0).astype(BF16)
            q = qT_ref[0, qi, hd * DH:(hd + 1) * DH, :]
            parts = [q, zeros] if hd % 2 == 0 else [zeros, q]
            qpad_ref[(qi % 2) * HPS + hd] = jnp.concatenate(parts + [ones], axis=0)

    def init():
        m_ref[...] = jnp.full(m_ref.shape, -1e30, F32)
        acc_ref[...] = jnp.zeros(acc_ref.shape, F32)

    def finish(qi):
        oT = jnp.concatenate(
            [acc_ref[hd, :DH, :] / acc_ref[hd, DH:DH + 1, :] for hd in range(HPS)], axis=0)
        o_ref[0, pl.ds(pl.multiple_of(qi * TQ, TQ), TQ), :] = oT.T.astype(BF16)
        init()

    def scores(t, s_ref, q):
        k0 = pl.multiple_of(t * TK, TK)
        aug = aug_ref[0, pl.ds(k0, TK), :]
        for hd in range(HPS):
            pair = slice((hd // 2) * LANES, (hd // 2 + 1) * LANES)
            kk = jnp.concatenate([k_ref[0, pl.ds(k0, TK), pair], aug], axis=1)
            s_ref[hd] = _dot(kk, qpad_ref[(q % 2) * HPS + hd])

    def update(t, s_ref, masked):
        for hd in range(HPS):
            sT = s_ref[hd]
            if masked:
                sT = jnp.where(key_le_query, sT, -jnp.inf)
            m_old = m_ref[hd]
            m_new = jnp.maximum(m_old, jnp.max(sT, axis=0, keepdims=True))
            p = jnp.exp2(sT - m_new).astype(BF16)
            va = jnp.concatenate([vT_ref[0, t, hd * DH:(hd + 1) * DH, :], ones_rows], axis=0)
            acc_ref[hd] = acc_ref[hd] * jnp.exp2(m_old - m_new) + _dot(va, p)
            m_ref[hd] = m_new

    def tiles(base, count, q):
        for i in range(0, count, 2):
            scores(base + i + 1, sb_ref, q)
            update(base + i, sa_ref, False)
            scores(base + i + 2, sa_ref, q)
            update(base + i + 1, sb_ref, False)

    n_q = qT_ref.shape[1]

    def query_tile(qi, _):
        nxt = jnp.minimum(qi + 1, n_q - 1)
        n_main = qi // KT_UNROLL

        def main(j, _):
            tiles(j * KT_UNROLL, KT_UNROLL, qi)
            return 0

        lax.fori_loop(0, n_main, main, 0)
        left = qi - n_main * KT_UNROLL

        def pair(j, _):
            tiles(n_main * KT_UNROLL + 2 * j, 2, qi)
            return 0

        lax.fori_loop(0, left // 2, pair, 0)

        @pl.when(qi % 2 == 0)
        def _():
            update(qi, sa_ref, True)
            build_qpad(nxt)
            scores(0, sa_ref, nxt)
            finish(qi)

        @pl.when(qi % 2 == 1)
        def _():
            scores(qi, sb_ref, qi)
            update(qi - 1, sa_ref, False)
            build_qpad(nxt)
            scores(0, sa_ref, nxt)
            update(qi, sb_ref, True)
            finish(qi)

        return 0

    build_qpad(0)
    init()
    scores(0, sa_ref, 0)
    lax.fori_loop(0, n_q, query_tile, 0)


def _fox_attn(qT, k2, aug, vT):
    bsz, nq, _, _ = qT.shape
    seq = k2.shape[1]
    assert TQ == TK
    gw = HPS * DH
    return pl.pallas_call(
        _fox_attn_kernel,
        grid=(bsz, N_HEADS // HPS),
        in_specs=[
            pl.BlockSpec((1, nq, gw, TQ), lambda b, g: (b, 0, g, 0)),
            pl.BlockSpec((1, seq, gw), lambda b, g: (b, 0, g)),
            pl.BlockSpec((1, seq, LANES), lambda b, g: (b, 0, 0)),
            pl.BlockSpec((1, seq // TK, gw, TK), lambda b, g: (b, 0, g, 0)),
        ],
        out_specs=pl.BlockSpec((1, seq, gw), lambda b, g: (b, 0, g)),
        out_shape=jax.ShapeDtypeStruct((bsz, seq, D), BF16),
        scratch_shapes=[
            pltpu.VMEM((2 * HPS, MXU, TQ), BF16),
            pltpu.VMEM((HPS, TK, TQ), F32),
            pltpu.VMEM((HPS, TK, TQ), F32),
            pltpu.VMEM((HPS, 1, TQ), F32),
            pltpu.VMEM((HPS, DH + BF16_ROWS, TQ), F32),
        ],
        compiler_params=_params(),
        name="fox_attn",
    )(qT, k2, aug, vT)


def _fox_tail_kernel(x_ref, o_ref_in, mod_ref, nmlp_ref, wout_ref, w1_ref, w2_ref, out_ref):
    x = x_ref[0]
    mod = mod_ref[0]
    x1 = x + mod[2:3] * _dot(o_ref_in[0], wout_ref[...])
    out_ref[0] = _mlp_tail(x1, mod, nmlp_ref[...], w1_ref, w2_ref)


def _fox_tail(x, o, mod, nmlp, w_out, w1_all, w2_all, layer):
    bsz, seq, _ = x.shape
    tok = pl.BlockSpec((1, TM, D), lambda b, s: (b, s, 0))
    return pl.pallas_call(
        _fox_tail_kernel,
        grid=(bsz, seq // TM),
        in_specs=[
            tok, tok,
            pl.BlockSpec((1, 6, D), lambda b, s: (b, 0, 0)),
            _const_spec((1, D)),
            _const_spec((D, D)), _layer_spec(w1_all.shape, layer), _layer_spec(w2_all.shape, layer),
        ],
        out_specs=tok,
        out_shape=jax.ShapeDtypeStruct(x.shape, F32),
        compiler_params=_params(),
        name="fox_tail",
    )(x, o, mod, nmlp.reshape(1, D), w_out.astype(BF16), w1_all, w2_all)


def kernel(x, c, norm_mix, norm_mlp, w_ada, b_ada, w_mlp_in, w_mlp_out, fox_w_in, fox_b_f, fox_q_norm, fox_k_norm, fox_w_out, sg_w_in, sg_ln_g, sg_ln_b, sg_w_s, sg_b_s, sg_w_out, cv_w_pw1, cv_b_pw1, cv_w_dw, cv_b_dw, cv_ln_g, cv_ln_b, cv_w_pw2, cv_b_pw2):
    depth = w_ada.shape[0]
    assert x.shape[1] % TM == 0 and x.shape[2] == D
    mods = _ada(c, w_ada, b_ada)
    w1_all = w_mlp_in.astype(BF16)
    w2_all = w_mlp_out.astype(BF16)
    for i in range(depth):
        kind, j = i % 3, i // 3
        mod = mods[i]
        if kind == 0:
            qT, vT, k2, aug = _fox_proj(x, mod, norm_mix[i], fox_w_in[j], fox_b_f[j],
                                        fox_q_norm[j], fox_k_norm[j])
            o = _fox_attn(qT, k2, aug, vT)
            x = _fox_tail(x, o, mod, norm_mlp[i], fox_w_out[j], w1_all, w2_all, i)
        elif kind == 1:
            x = _sg_layer(x, mod, norm_mix[i], norm_mlp[i], sg_w_in[j], sg_ln_g[j], sg_ln_b[j],
                          sg_w_s[j], sg_b_s[j], sg_w_out[j], w1_all, w2_all, i)
        else:
            x = _cv_layer(x, mod, norm_mix[i], norm_mlp[i], cv_w_pw1[j], cv_b_pw1[j], cv_w_dw[j],
                          cv_b_dw[j], cv_ln_g[j], cv_ln_b[j], cv_w_pw2[j], cv_b_pw2[j],
                          w1_all[i], w2_all[i])
    return x
```

```python
import functools

import jax
import jax.numpy as jnp
import numpy as np
from jax import lax
from jax.experimental import pallas as pl
from jax.experimental.pallas import tpu as pltpu

F32 = jnp.float32
BF16 = jnp.bfloat16

D = 1024
N_HEADS = 16
DH = D // N_HEADS
D_FF = 4 * D
EPS = 1e-6
LOG2E = 1.4426950408889634
SG_CHUNK = 128
SG_BLOCK = 64
SG_GROUPS = 8
CONV_W = 31
HALO = 32

LANES = 128
MXU = 256
TM = 512
TQ = 256
TK = 256
HPS = 8
KT_UNROLL = 4
FF_CHUNK = 1024
CONV_ROWS = 128
CV_LT_PER_CHUNK = 2
BF16_ROWS = 16
ADA_TN = 3072
N_SPLIT = 3
VMEM_LIMIT = 56 * 1024 * 1024


def _const_spec(shape):
    nd = len(shape)
    return pl.BlockSpec(shape, lambda *_: (0,) * nd, pipeline_mode=pl.Buffered(1))


def _layer_spec(shape, layer):
    nd = len(shape)
    return pl.BlockSpec((1,) + tuple(shape[1:]), lambda *_: (layer,) + (0,) * (nd - 1),
                        pipeline_mode=pl.Buffered(1))


def _params():
    return pltpu.CompilerParams(
        dimension_semantics=("arbitrary", "arbitrary"), vmem_limit_bytes=VMEM_LIMIT)


def _dot(a, b):
    return jnp.dot(a, b, preferred_element_type=F32)


def _sigmoid(x):
    return 1.0 / (1.0 + jnp.exp(-x))


def _rms_mod(x, gain, scale, shift):
    ms = jnp.mean(x * x, axis=-1, keepdims=True)
    return (x * lax.rsqrt(ms + EPS) * gain) * (1.0 + scale) + shift


def _layer_norm(x, g, b):
    mu = jnp.mean(x, axis=-1, keepdims=True)
    xc = x - mu
    var = jnp.mean(xc * xc, axis=-1, keepdims=True)
    return xc * lax.rsqrt(var + EPS) * g + b


def _split_bf16(x):
    pieces = []
    r = x
    for _ in range(N_SPLIT):
        p = r.astype(BF16)
        pieces.append(p)
        r = r - p.astype(F32)
    return pieces


def _mlp_tail(x1, mod, nmlp, w1_ref, w2_ref):
    w1_ref, w2_ref = w1_ref.at[0], w2_ref.at[0]
    h = _rms_mod(x1, nmlp, mod[4:5], mod[3:4]).astype(BF16)
    acc = jnp.zeros(x1.shape, F32)
    for c in range(D_FF // FF_CHUNK):
        lo = c * FF_CHUNK
        hid = _dot(h, w1_ref[:, lo:lo + FF_CHUNK])
        hid = jnp.square(jnp.maximum(hid, 0.0)).astype(BF16)
        acc = acc + _dot(hid, w2_ref[lo:lo + FF_CHUNK, :])
    return x1 + mod[5:6] * acc


def _ada_kernel(c_ref, w_ref, b_ref, o_ref):
    c = c_ref[...]
    rows = c.shape[0]
    ca = _split_bf16(c * _sigmoid(c))
    w = w_ref[0]
    w_hi = w.astype(BF16)
    w_lo = (w - w_hi.astype(F32)).astype(BF16)
    r_hi = _dot(jnp.concatenate(ca, axis=0), w_hi)
    r_lo = _dot(jnp.concatenate(ca[:2], axis=0), w_lo)
    out = b_ref[0] + r_lo[0:rows] + r_lo[rows:2 * rows]
    for i in range(N_SPLIT):
        out = out + r_hi[i * rows:(i + 1) * rows]
    o_ref[0] = out


def _ada(c, w_ada, b_ada):
    depth, _, n = w_ada.shape
    bsz = c.shape[0]
    rows = BF16_ROWS
    tn = ADA_TN
    assert bsz <= rows and n % tn == 0
    cp = jnp.zeros((rows, D), F32).at[:bsz].set(c)
    out = pl.pallas_call(
        _ada_kernel,
        grid=(depth, n // tn),
        in_specs=[
            pl.BlockSpec((rows, D), lambda i, j: (0, 0)),
            pl.BlockSpec((1, D, tn), lambda i, j: (i, 0, j)),
            pl.BlockSpec((1, 1, tn), lambda i, j: (i, 0, j)),
        ],
        out_specs=pl.BlockSpec((1, rows, tn), lambda i, j: (i, 0, j)),
        out_shape=jax.ShapeDtypeStruct((depth, rows, n), F32),
        compiler_params=_params(),
        name="ada_mod",
    )(cp, w_ada, b_ada.reshape(depth, 1, n))
    return out[:, :bsz].reshape(depth, bsz, 6, D)


def _sg_kernel(x_ref, mod_ref, nmix_ref, nmlp_ref, win_ref, lng_ref, lnb_ref, ws_ref, bs_ref,
               wout_ref, w1_ref, w2_ref, o_ref, gated_ref):
    x = x_ref[0]
    mod = mod_ref[0]
    h = _rms_mod(x, nmix_ref[...], mod[1:2], mod[0:1]).astype(BF16)
    uv = jax.nn.gelu(_dot(h, win_ref[...]), approximate=True)
    u = uv[:, :D]
    v = _layer_norm(uv[:, D:], lng_ref[...], lnb_ref[...]).astype(BF16)

    t = lax.broadcasted_iota(jnp.int32, (SG_CHUNK, SG_CHUNK), 0) // SG_BLOCK
    s = lax.broadcasted_iota(jnp.int32, (SG_CHUNK, SG_CHUNK), 1) // SG_BLOCK
    causal = s <= t
    gd = D // SG_GROUPS
    for g in range(SG_GROUPS):
        ws = jnp.where(causal, ws_ref[g], 0.0).astype(BF16)
        bias = bs_ref[g]
        cols = slice(g * gd, (g + 1) * gd)
        for j in range(TM // (2 * SG_CHUNK)):
            r0 = j * 2 * SG_CHUNK
            r1 = r0 + SG_CHUNK
            r2 = r1 + SG_CHUNK
            rhs = jnp.concatenate([v[r0:r1, cols], v[r1:r2, cols]], axis=1)
            mix = _dot(ws, rhs)
            gated_ref[r0:r1, cols] = (u[r0:r1, cols] * (mix[:, :gd] + bias)).astype(BF16)
            gated_ref[r1:r2, cols] = (u[r1:r2, cols] * (mix[:, gd:] + bias)).astype(BF16)

    x1 = x + mod[2:3] * _dot(gated_ref[...], wout_ref[...])
    o_ref[0] = _mlp_tail(x1, mod, nmlp_ref[...], w1_ref, w2_ref)


def _sg_layer(x, mod, nmix, nmlp, w_in, ln_g, ln_b, w_s, b_s, w_out, w1_all, w2_all, layer):
    bsz, seq, _ = x.shape
    tok = pl.BlockSpec((1, TM, D), lambda b, s: (b, s, 0))
    bs_x = jnp.broadcast_to(b_s[:, :, None], (SG_GROUPS, SG_CHUNK, D // SG_GROUPS))
    return pl.pallas_call(
        _sg_kernel,
        grid=(bsz, seq // TM),
        in_specs=[
            tok,
            pl.BlockSpec((1, 6, D), lambda b, s: (b, 0, 0)),
            _const_spec((1, D)), _const_spec((1, D)),
            _const_spec((D, 2 * D)), _const_spec((1, D)), _const_spec((1, D)),
            _const_spec((SG_GROUPS, SG_CHUNK, SG_CHUNK)),
            _const_spec((SG_GROUPS, SG_CHUNK, D // SG_GROUPS)),
            _const_spec((D, D)), _layer_spec(w1_all.shape, layer), _layer_spec(w2_all.shape, layer),
        ],
        out_specs=tok,
        out_shape=jax.ShapeDtypeStruct(x.shape, F32),
        scratch_shapes=[pltpu.VMEM((TM, D), BF16)],
        compiler_params=_params(),
        name="sg_layer",
    )(x, mod, nmix.reshape(1, D), nmlp.reshape(1, D), w_in.astype(BF16),
      ln_g.reshape(1, D), ln_b.reshape(1, D), w_s, bs_x, w_out.astype(BF16), w1_all, w2_all)


def _cv_kernel(n_s, x_ref, moda_ref, modb_ref, nmix_ref, nmlp_ref, wpw1_ref, bpw1_ref, wdw_ref,
               bdw_ref, lng_ref, lnb_ref, wpw2_ref, bpw2_ref, w1_ref, w2_ref, o_ref,
               ybuf_ref, conv_ref, xprev_ref, x1_ref, h2_ref, acc_ref):
    i = pl.program_id(0)
    n_lt = D // LANES

    @pl.when(i == 0)
    def _():
        conv_ref[...] = jnp.zeros(conv_ref.shape, F32)
        xprev_ref[...] = jnp.zeros(xprev_ref.shape, F32)

    @pl.when(i % n_s == 0)
    def _():
        ybuf_ref[:, 0:HALO, :] = jnp.zeros((n_lt, HALO, LANES), F32)

    modb = modb_ref[0]
    conv = jnp.concatenate([conv_ref[c] for c in range(n_lt)], axis=1)
    z = _layer_norm(conv, lng_ref[...], lnb_ref[...])
    z = (z * _sigmoid(z)).astype(BF16)
    x1 = xprev_ref[...] + modb[2:3] * (_dot(z, wpw2_ref[...]) + bpw2_ref[...])
    x1_ref[...] = x1
    h2_ref[...] = _rms_mod(x1, nmlp_ref[...], modb[4:5], modb[3:4]).astype(BF16)
    acc_ref[...] = jnp.zeros(acc_ref.shape, F32)

    x = x_ref[0]
    moda = moda_ref[0]
    h = _rms_mod(x, nmix_ref[...], moda[1:2], moda[0:1]).astype(BF16)
    yz = _dot(h, wpw1_ref[...]) + bpw1_ref[...]
    y = yz[:, :D] * _sigmoid(yz[:, D:])
    for c in range(n_lt):
        ybuf_ref[c, HALO:HALO + TM, :] = y[:, c * LANES:(c + 1) * LANES]
    xprev_ref[...] = x

    first = HALO - (CONV_W - 1)
    rb = CONV_ROWS
    win_rows = rb + HALO

    def conv_block(r, c):
        win = ybuf_ref[c, r * rb:r * rb + win_rows, :]
        acc = jnp.zeros((rb, LANES), F32) + bdw_ref[c]
        for b in range(8):
            phase = win if b == 0 else pltpu.roll(win, win_rows - b, axis=0)
            for j in range(CONV_W):
                off = first + j
                if off % 8 == b:
                    acc = acc + wdw_ref[c, j:j + 1, :] * phase[off - b:off - b + rb, :]
        conv_ref[c, r * rb:(r + 1) * rb, :] = acc

    def chunk(c, _):
        hid = jnp.maximum(_dot(h2_ref[...], w1_ref[c]).astype(BF16), 0.0)
        acc_ref[...] += _dot(hid * hid, w2_ref[c])
        for k in range(CV_LT_PER_CHUNK):
            for r in range(TM // rb):
                conv_block(r, c * CV_LT_PER_CHUNK + k)
        return 0

    lax.fori_loop(0, n_lt // CV_LT_PER_CHUNK, chunk, 0)
    o_ref[0] = x1_ref[...] + modb[5:6] * acc_ref[...]
    ybuf_ref[:, 0:HALO, :] = ybuf_ref[:, TM:TM + HALO, :]


def _cv_layer(x, mod, nmix, nmlp, w_pw1, b_pw1, w_dw, b_dw, ln_g, ln_b, w_pw2, b_pw2, w1, w2):
    bsz, seq, _ = x.shape
    n_s = seq // TM
    n = bsz * n_s
    n_lt = D // LANES
    n_ch = n_lt // CV_LT_PER_CHUNK
    ffc = D_FF // n_ch

    def cur(i):
        return jnp.minimum(i, n - 1)

    def prev(i):
        return jnp.maximum(i - 1, 0)

    wdw = jnp.zeros((HALO, D), F32).at[:CONV_W].set(w_dw).reshape(HALO, n_lt, LANES).transpose(1, 0, 2)
    bdw = b_dw.reshape(n_lt, 1, LANES)
    w1c = w1.astype(BF16).reshape(D, n_ch, ffc).transpose(1, 0, 2)
    w2c = w2.astype(BF16).reshape(n_ch, ffc, D)
    return pl.pallas_call(
        functools.partial(_cv_kernel, n_s),
        grid=(n + 1,),
        in_specs=[
            pl.BlockSpec((1, TM, D), lambda i: (cur(i) // n_s, cur(i) % n_s, 0)),
            pl.BlockSpec((1, 6, D), lambda i: (cur(i) // n_s, 0, 0)),
            pl.BlockSpec((1, 6, D), lambda i: (prev(i) // n_s, 0, 0)),
            _const_spec((1, D)), _const_spec((1, D)),
            _const_spec((D, 2 * D)), _const_spec((1, 2 * D)),
            _const_spec((n_lt, HALO, LANES)), _const_spec((n_lt, 1, LANES)),
            _const_spec((1, D)), _const_spec((1, D)),
            _const_spec((D, D)), _const_spec((1, D)),
            _const_spec((n_ch, D, ffc)), _const_spec((n_ch, ffc, D)),
        ],
        out_specs=pl.BlockSpec((1, TM, D), lambda i: (prev(i) // n_s, prev(i) % n_s, 0)),
        out_shape=jax.ShapeDtypeStruct(x.shape, F32),
        scratch_shapes=[
            pltpu.VMEM((n_lt, TM + HALO, LANES), F32),
            pltpu.VMEM((n_lt, TM, LANES), F32),
            pltpu.VMEM((TM, D), F32),
            pltpu.VMEM((TM, D), F32),
            pltpu.VMEM((TM, D), BF16),
            pltpu.VMEM((TM, D), F32),
        ],
        compiler_params=pltpu.CompilerParams(
            dimension_semantics=("arbitrary",), vmem_limit_bytes=VMEM_LIMIT),
        name="cv_layer",
    )(x, mod, mod, nmix.reshape(1, D), nmlp.reshape(1, D), w_pw1.astype(BF16),
      b_pw1.reshape(1, 2 * D), wdw, bdw, ln_g.reshape(1, D), ln_b.reshape(1, D),
      w_pw2.astype(BF16), b_pw2.reshape(1, D), w1c, w2c)


def _fox_proj_kernel(x_ref, mod_ref, nmix_ref, w_ref, wf_ref, bf_ref, gqk_ref, gsum_ref,
                     tri_ref, qT_ref, vT_ref, k2_ref, aug_ref, carry_ref):
    @pl.when(pl.program_id(1) == 0)
    def _():
        carry_ref[...] = jnp.zeros(carry_ref.shape, F32)

    x = x_ref[0]
    mod = mod_ref[0]
    h = _rms_mod(x, nmix_ref[...], mod[1:2], mod[0:1]).astype(BF16)

    f_pre = _dot(h, wf_ref[...]) + bf_ref[...]
    log_f = jnp.minimum(f_pre, 0.0) - jnp.log1p(jnp.exp(-jnp.abs(f_pre)))
    cum = carry_ref[0:1, :]
    for p in _split_bf16(log_f):
        cum = cum + _dot(tri_ref[...], p)
    carry_ref[0:1, :] = cum[TM - 1:TM, :]

    lane = lax.broadcasted_iota(jnp.int32, (TM, LANES), 1)
    aug = jnp.zeros((TM, LANES), BF16)
    for i, p in enumerate(_split_bf16(cum * -LOG2E)):
        aug = jnp.where((lane >= N_HEADS * i) & (lane < N_HEADS * (i + 1)), p, aug)
    aug_ref[0] = aug

    tn = (((0,), (1,)), ((), ()))
    qT = lax.dot_general(w_ref[:, 0:D], h, tn, preferred_element_type=F32).reshape(N_HEADS, DH, TM)
    qT = qT * lax.rsqrt(jnp.mean(qT * qT, axis=1, keepdims=True) + EPS)
    qT = qT.reshape(D, TM).astype(BF16)
    vT = lax.dot_general(w_ref[:, 2 * D:3 * D], h, tn, preferred_element_type=F32).astype(BF16)
    for j in range(TM // TQ):
        qT_ref[0, j] = qT[:, j * TQ:(j + 1) * TQ]
    for j in range(TM // TK):
        vT_ref[0, j] = vT[:, j * TK:(j + 1) * TK]

    k = _dot(h, w_ref[:, D:2 * D])
    kk = (k * k).astype(BF16)
    ss = jnp.concatenate(
        [_dot(kk[:, c * MXU:(c + 1) * MXU], gsum_ref[...]) for c in range(D // MXU)], axis=1)
    k2_ref[0] = (k * lax.rsqrt(ss * (1.0 / DH) + EPS) * gqk_ref[...]).astype(BF16)


def _fox_consts():
    g = np.kron(np.eye(MXU // DH), np.ones((DH, DH)))
    tri = np.tril(np.ones((TM, TM)))
    return jnp.asarray(g, BF16), jnp.asarray(tri, BF16)


def _fox_proj(x, mod, nmix, w_in, b_f, q_g, k_g):
    bsz, seq, _ = x.shape
    gsum, tri = _fox_consts()
    n_f = N_SPLIT * N_HEADS
    wf = jnp.zeros((D, LANES), F32).at[:, :n_f].set(jnp.tile(w_in[:, 3 * D:], (1, N_SPLIT))).astype(BF16)
    bf = jnp.zeros((1, LANES), F32).at[0, :n_f].set(jnp.tile(b_f, N_SPLIT))
    gqk = jnp.tile(q_g * k_g * (LOG2E * DH ** -0.5), N_HEADS).reshape(1, D)
    tok = pl.BlockSpec((1, TM, D), lambda b, s: (b, s, 0))
    return pl.pallas_call(
        _fox_proj_kernel,
        grid=(bsz, seq // TM),
        in_specs=[
            tok,
            pl.BlockSpec((1, 6, D), lambda b, s: (b, 0, 0)),
            _const_spec((1, D)),
            _const_spec(w_in.shape), _const_spec((D, LANES)),
            _const_spec((1, LANES)), _const_spec((1, D)), _const_spec((MXU, MXU)),
            _const_spec((TM, TM)),
        ],
        out_specs=[
            pl.BlockSpec((1, TM // TQ, D, TQ), lambda b, s: (b, s, 0, 0)),
            pl.BlockSpec((1, TM // TK, D, TK), lambda b, s: (b, s, 0, 0)),
            tok,
            pl.BlockSpec((1, TM, LANES), lambda b, s: (b, s, 0)),
        ],
        out_shape=[
            jax.ShapeDtypeStruct((bsz, seq // TQ, D, TQ), BF16),
            jax.ShapeDtypeStruct((bsz, seq // TK, D, TK), BF16),
            jax.ShapeDtypeStruct((bsz, seq, D), BF16),
            jax.ShapeDtypeStruct((bsz, seq, LANES), BF16),
        ],
        scratch_shapes=[pltpu.VMEM((8, LANES), F32)],
        compiler_params=_params(),
        name="fox_proj",
    )(x, mod, nmix.reshape(1, D), w_in.astype(BF16), wf, bf, gqk, gsum, tri)


def _fox_attn_kernel(qT_ref, k_ref, aug_ref, vT_ref, o_ref, qpad_ref, sa_ref, sb_ref, m_ref, acc_ref):
    head0 = pl.program_id(1) * HPS
    row = lax.broadcasted_iota(jnp.int32, (LANES, TQ), 0)
    zeros = jnp.zeros((DH, TQ), BF16)
    ones_rows = jnp.ones((BF16_ROWS, TK), BF16)
    key_le_query = (lax.broadcasted_iota(jnp.int32, (TK, TQ), 0)
                    <= lax.broadcasted_iota(jnp.int32, (TK, TQ), 1))

    def build_qpad(qi):
        for hd in range(HPS):
            pick = row == head0 + hd
            for i in range(1, N_SPLIT):
                pick = pick | (row == head0 + hd + N_HEADS * i)
            ones = jnp.where(pick, 1.0, 0.0).astype(BF16)
            q = qT_ref[0, qi, hd * DH:(hd + 1) * DH, :]
            parts = [q, zeros] if hd % 2 == 0 else [zeros, q]
            qpad_ref[(qi % 2) * HPS + hd] = jnp.concatenate(parts + [ones], axis=0)

    def init():
        m_ref[...] = jnp.full(m_ref.shape, -1e30, F32)
        acc_ref[...] = jnp.zeros(acc_ref.shape, F32)

    def finish(qi):
        oT = jnp.concatenate(
            [acc_ref[hd, :DH, :] / acc_ref[hd, DH:DH + 1, :] for hd in range(HPS)], axis=0)
        o_ref[0, pl.ds(pl.multiple_of(qi * TQ, TQ), TQ), :] = oT.T.astype(BF16)
        init()

    def scores(t, s_ref, q):
        k0 = pl.multiple_of(t * TK, TK)
        aug = aug_ref[0, pl.ds(k0, TK), :]
        for hd in range(HPS):
            pair = slice((hd // 2) * LANES, (hd // 2 + 1) * LANES)
            kk = jnp.concatenate([k_ref[0, pl.ds(k0, TK), pair], aug], axis=1)
            s_ref[hd] = _dot(kk, qpad_ref[(q % 2) * HPS + hd])

    def update(t, s_ref, masked):
        for hd in range(HPS):
            sT = s_ref[hd]
            if masked:
                sT = jnp.where(key_le_query, sT, -jnp.inf)
            m_old = m_ref[hd]
            m_new = jnp.maximum(m_old, jnp.max(sT, axis=0, keepdims=True))
            p = jnp.exp2(sT - m_new).astype(BF16)
            va = jnp.concatenate([vT_ref[0, t, hd * DH:(hd + 1) * DH, :], ones_rows], axis=0)
            acc_ref[hd] = acc_ref[hd] * jnp.exp2(m_old - m_new) + _dot(va, p)
            m_ref[hd] = m_new

    def tiles(base, count, q):
        for i in range(0, count, 2):
            scores(base + i + 1, sb_ref, q)
            update(base + i, sa_ref, False)
            scores(base + i + 2, sa_ref, q)
            update(base + i + 1, sb_ref, False)

    n_q = qT_ref.shape[1]

    def query_tile(qi, _):
        nxt = jnp.minimum(qi + 1, n_q - 1)
        n_main = qi // KT_UNROLL

        def main(j, _):
            tiles(j * KT_UNROLL, KT_UNROLL, qi)
            return 0

        lax.fori_loop(0, n_main, main, 0)
        left = qi - n_main * KT_UNROLL

        def pair(j, _):
            tiles(n_main * KT_UNROLL + 2 * j, 2, qi)
            return 0

        lax.fori_loop(0, left // 2, pair, 0)

        @pl.when(qi % 2 == 0)
        def _():
            update(qi, sa_ref, True)
            build_qpad(nxt)
            scores(0, sa_ref, nxt)
            finish(qi)

        @pl.when(qi % 2 == 1)
        def _():
            scores(qi, sb_ref, qi)
            update(qi - 1, sa_ref, False)
            build_qpad(nxt)
            scores(0, sa_ref, nxt)
            update(qi, sb_ref, True)
            finish(qi)

        return 0

    build_qpad(0)
    init()
    scores(0, sa_ref, 0)
    lax.fori_loop(0, n_q, query_tile, 0)


def _fox_attn(qT, k2, aug, vT):
    bsz, nq, _, _ = qT.shape
    seq = k2.shape[1]
    assert TQ == TK
    gw = HPS * DH
    return pl.pallas_call(
        _fox_attn_kernel,
        grid=(bsz, N_HEADS // HPS),
        in_specs=[
            pl.BlockSpec((1, nq, gw, TQ), lambda b, g: (b, 0, g, 0)),
            pl.BlockSpec((1, seq, gw), lambda b, g: (b, 0, g)),
            pl.BlockSpec((1, seq, LANES), lambda b, g: (b, 0, 0)),
            pl.BlockSpec((1, seq // TK, gw, TK), lambda b, g: (b, 0, g, 0)),
        ],
        out_specs=pl.BlockSpec((1, seq, gw), lambda b, g: (b, 0, g)),
        out_shape=jax.ShapeDtypeStruct((bsz, seq, D), BF16),
        scratch_shapes=[
            pltpu.VMEM((2 * HPS, MXU, TQ), BF16),
            pltpu.VMEM((HPS, TK, TQ), F32),
            pltpu.VMEM((HPS, TK, TQ), F32),
            pltpu.VMEM((HPS, 1, TQ), F32),
            pltpu.VMEM((HPS, DH + BF16_ROWS, TQ), F32),
        ],
        compiler_params=_params(),
        name="fox_attn",
    )(qT, k2, aug, vT)


def _fox_tail_kernel(x_ref, o_ref_in, mod_ref, nmlp_ref, wout_ref, w1_ref, w2_ref, out_ref):
    x = x_ref[0]
    mod = mod_ref[0]
    x1 = x + mod[2:3] * _dot(o_ref_in[0], wout_ref[...])
    out_ref[0] = _mlp_tail(x1, mod, nmlp_ref[...], w1_ref, w2_ref)


def _fox_tail(x, o, mod, nmlp, w_out, w1_all, w2_all, layer):
    bsz, seq, _ = x.shape
    tok = pl.BlockSpec((1, TM, D), lambda b, s: (b, s, 0))
    return pl.pallas_call(
        _fox_tail_kernel,
        grid=(bsz, seq // TM),
        in_specs=[
            tok, tok,
            pl.BlockSpec((1, 6, D), lambda b, s: (b, 0, 0)),
            _const_spec((1, D)),
            _const_spec((D, D)), _layer_spec(w1_all.shape, layer), _layer_spec(w2_all.shape, layer),
        ],
        out_specs=tok,
        out_shape=jax.ShapeDtypeStruct(x.shape, F32),
        compiler_params=_params(),
        name="fox_tail",
    )(x, o, mod, nmlp.reshape(1, D), w_out.astype(BF16), w1_all, w2_all)


def kernel(x, c, norm_mix, norm_mlp, w_ada, b_ada, w_mlp_in, w_mlp_out, fox_w_in, fox_b_f, fox_q_norm, fox_k_norm, fox_w_out, sg_w_in, sg_ln_g, sg_ln_b, sg_w_s, sg_b_s, sg_w_out, cv_w_pw1, cv_b_pw1, cv_w_dw, cv_b_dw, cv_ln_g, cv_ln_b, cv_w_pw2, cv_b_pw2):
    depth = w_ada.shape[0]
    assert x.shape[1] % TM == 0 and x.shape[2] == D
    mods = _ada(c, w_ada, b_ada)
    w1_all = w_mlp_in.astype(BF16)
    w2_all = w_mlp_out.astype(BF16)
    for i in range(depth):
        kind, j = i % 3, i // 3
        mod = mods[i]
        if kind == 0:
            qT, vT, k2, aug = _fox_proj(x, mod, norm_mix[i], fox_w_in[j], fox_b_f[j],
                                        fox_q_norm[j], fox_k_norm[j])
            o = _fox_attn(qT, k2, aug, vT)
            x = _fox_tail(x, o, mod, norm_mlp[i], fox_w_out[j], w1_all, w2_all, i)
        elif kind == 1:
            x = _sg_layer(x, mod, norm_mix[i], norm_mlp[i], sg_w_in[j], sg_ln_g[j], sg_ln_b[j],
                          sg_w_s[j], sg_b_s[j], sg_w_out[j], w1_all, w2_all, i)
        else:
            x = _cv_layer(x, mod, norm_mix[i], norm_mlp[i], cv_w_pw1[j], cv_b_pw1[j], cv_w_dw[j],
                          cv_b_dw[j], cv_ln_g[j], cv_ln_b[j], cv_w_pw2[j], cv_b_pw2[j],
                          w1_all[i], w2_all[i])
    return x
```
